```python
import math
import jax, jax.numpy as jnp
from jax import lax
import numpy as np


D_MODEL = 1024
BATCH = 8
SEQ = 4096
DEPTH = 2

GRID_W = 64
CTX_LEN = 256
N_MIXERS = 4
GROUP_WIDTH = D_MODEL // N_MIXERS
MIX_WIDTH = N_MIXERS * GROUP_WIDTH
Q_BLOCK = 128
CHUNK = 128
ROPE_THETA = 10000.0
NORM_EPS = 1e-6
FFN_RES = 0.5
D_FF = ((8 * D_MODEL // 3 + 127) // 128) * 128
N_MOD = 9

A_HEADS = 4
A_V_DIM = GROUP_WIDTH // A_HEADS
A_QK_DIM = A_V_DIM // 2
B_HEADS = 4
B_KV_HEADS = 2
B_HEAD_DIM = GROUP_WIDTH // B_HEADS
C_GROUPS = 4
C_GROUP_DIM = GROUP_WIDTH // C_GROUPS
D_HEADS = 4
D_V_DIM = GROUP_WIDTH // D_HEADS
D_NOPE_DIM = D_V_DIM
D_ROPE_DIM = D_V_DIM // 2
D_Q_RANK = D_MODEL // 4
D_KV_RANK = D_MODEL // 8

IN_SIZES = (A_HEADS * 2 * A_QK_DIM, A_HEADS * 2 * A_QK_DIM, A_HEADS * A_V_DIM,
            B_HEADS * B_HEAD_DIM, B_KV_HEADS * B_HEAD_DIM, B_KV_HEADS * B_HEAD_DIM,
            GROUP_WIDTH, GROUP_WIDTH,
            D_Q_RANK, D_KV_RANK, D_ROPE_DIM)
IN_WIDTH = (2 * A_HEADS * 2 * A_QK_DIM + A_HEADS * A_V_DIM + B_HEADS * B_HEAD_DIM
            + 2 * B_KV_HEADS * B_HEAD_DIM + 2 * GROUP_WIDTH + D_Q_RANK + D_KV_RANK + D_ROPE_DIM)

kernel_name = 'hybrid_parallel_group_dit_trunk'


def rms_norm(x, g):
    xf = x.astype(jnp.float32)
    y = xf * lax.rsqrt(jnp.mean(xf * xf, axis=-1, keepdims=True) + NORM_EPS)
    return (y * g.astype(jnp.float32)).astype(x.dtype)


def layer_norm(x, g, b):
    xf = x.astype(jnp.float32)
    mu = jnp.mean(xf, axis=-1, keepdims=True)
    xc = xf - mu
    y = xc * lax.rsqrt(jnp.mean(xc * xc, axis=-1, keepdims=True) + NORM_EPS)
    return (y * g.astype(jnp.float32) + b.astype(jnp.float32)).astype(x.dtype)


def split_cols(p, sizes):
    idx = [int(i) for i in np.cumsum(sizes)[:-1]]
    return jnp.split(p, idx, axis=-1)


def swiglu(h, w1, w2):
    g, u = jnp.split(h @ w1, 2, axis=-1)
    return (jax.nn.silu(g) * u) @ w2


def lambda_init(layer_idx):
    return 0.8 - 0.6 * math.exp(-0.3 * layer_idx)


def axial_rope_tables(n_rows, rot_dim):
    rows = jnp.repeat(jnp.arange(n_rows, dtype=jnp.float32), GRID_W)
    cols = jnp.tile(jnp.arange(GRID_W, dtype=jnp.float32), n_rows)
    axis_dim = rot_dim // 2
    inv_freq = ROPE_THETA ** (-jnp.arange(0, axis_dim, 2, dtype=jnp.float32) / axis_dim)
    ang_r = rows[:, None] * inv_freq[None, :]
    ang_c = cols[:, None] * inv_freq[None, :]
    return (jnp.cos(ang_r), jnp.sin(ang_r), jnp.cos(ang_c), jnp.sin(ang_c))


def apply_rope_axis(x, cos, sin):
    half = x.shape[-1] // 2
    shp = (1, x.shape[1]) + (1,) * (x.ndim - 3) + (half,)
    cos = cos.reshape(shp)
    sin = sin.reshape(shp)
    xf = x.astype(jnp.float32)
    x1, x2 = xf[..., :half], xf[..., half:]
    return jnp.concatenate([x1 * cos - x2 * sin, x2 * cos + x1 * sin], axis=-1).astype(x.dtype)


def apply_rope_2d(x, tabs):
    cr, sr, cc, sc = tabs
    a = x.shape[-1] // 2
    return jnp.concatenate([apply_rope_axis(x[..., :a], cr, sr),
                            apply_rope_axis(x[..., a:], cc, sc)], axis=-1)


def sweep_query_blocks(fn, q):
    b, s = q.shape[:2]
    nb = s // Q_BLOCK
    qb = jnp.moveaxis(q.reshape((b, nb, Q_BLOCK) + q.shape[2:]), 1, 0)
    out = jnp.moveaxis(lax.map(fn, qb), 0, 1)
    return out.reshape((b, s) + out.shape[3:])


def diff_core(q, k, v, lam, scale):
    s = jnp.einsum('bqhmd,bkhmd->bhmqk', q, k).astype(jnp.float32) * scale
    p = jax.nn.softmax(s, axis=-1)
    w = p[:, :, 0] - lam * p[:, :, 1]
    return jnp.einsum('bhqk,bkhd->bqhd', w.astype(v.dtype), v)


def gqa_core(q, k, v, scale):
    s = jnp.einsum('bqhgd,bkhd->bhgqk', q, k).astype(jnp.float32) * scale
    p = jax.nn.softmax(s, axis=-1).astype(v.dtype)
    return jnp.einsum('bhgqk,bkhd->bqhgd', p, v)


def diff_attention_group(pc, pl, lam_vecs, g_subln, lam_init, rope, need_ctx):
    def heads(q, k, v):
        b, s = q.shape[:2]
        return (q.reshape(b, s, A_HEADS, 2, A_QK_DIM), k.reshape(b, s, A_HEADS, 2, A_QK_DIM),
                v.reshape(b, s, A_HEADS, A_V_DIM))
    qc, kc, vc = heads(*pc)
    ql, kl, vl = heads(*pl)
    ql = apply_rope_2d(ql, rope)
    kl = apply_rope_2d(kl, rope)
    lv = lam_vecs.astype(jnp.float32)
    lam = jnp.exp(jnp.sum(lv[0] * lv[1])) - jnp.exp(jnp.sum(lv[2] * lv[3])) + lam_init
    scale = A_QK_DIM ** -0.5
    k_all = jnp.concatenate([kc, kl], axis=1)
    v_all = jnp.concatenate([vc, vl], axis=1)
    o_lat = sweep_query_blocks(lambda qb: diff_core(qb, k_all, v_all, lam, scale), ql)

    def finish(o):
        b, s = o.shape[:2]
        return (rms_norm(o, g_subln) * (1.0 - lam_init)).reshape(b, s, A_HEADS * A_V_DIM)
    o_ctx = finish(diff_core(qc, kc, vc, lam, scale)) if need_ctx else None
    return o_ctx, finish(o_lat)


def gqa_group(pc, pl, g_qn, g_kn, rope, need_ctx):
    def heads(q, k, v):
        b, s = q.shape[:2]
        q = rms_norm(q.reshape(b, s, B_HEADS, B_HEAD_DIM), g_qn)
        k = rms_norm(k.reshape(b, s, B_KV_HEADS, B_HEAD_DIM), g_kn)
        return q, k, v.reshape(b, s, B_KV_HEADS, B_HEAD_DIM)
    qc, kc, vc = heads(*pc)
    ql, kl, vl = heads(*pl)
    ql = apply_rope_2d(ql, rope)
    kl = apply_rope_2d(kl, rope)
    n_rep = B_HEADS // B_KV_HEADS

    def group(q):
        return q.reshape(q.shape[:2] + (B_KV_HEADS, n_rep, B_HEAD_DIM))

    def flat(o):
        return o.reshape(o.shape[:2] + (B_HEADS * B_HEAD_DIM,))
    scale = B_HEAD_DIM ** -0.5
    k_all = jnp.concatenate([kc, kl], axis=1)
    v_all = jnp.concatenate([vc, vl], axis=1)
    o_lat = sweep_query_blocks(lambda qb: gqa_core(qb, k_all, v_all, scale), group(ql))
    o_ctx = flat(gqa_core(group(qc), kc, vc, scale)) if need_ctx else None
    return o_ctx, flat(o_lat)


def chunk_gmlp(u, v, w_sp, b_sp, ln_g, ln_b):
    u = jax.nn.gelu(u)
    v = layer_norm(jax.nn.gelu(v), ln_g, ln_b)
    b, s, _ = v.shape
    vr = v.reshape(b, s // CHUNK, CHUNK, C_GROUPS, C_GROUP_DIM)
    mixed = jnp.einsum('gpq,bnqgc->bnpgc', w_sp, vr) + b_sp.T[None, None, :, :, None]
    return u * mixed.reshape(b, s, C_GROUPS * C_GROUP_DIM)


def mla_group(pc, pl, g_qa, w_uq, g_kva, w_ukv, rope, need_ctx):
    def expand(cq, ckv, kr):
        b, s = cq.shape[:2]
        q = (rms_norm(cq, g_qa) @ w_uq).reshape(b, s, D_HEADS, D_NOPE_DIM + D_ROPE_DIM)
        kv = (rms_norm(ckv, g_kva) @ w_ukv).reshape(b, s, D_HEADS, D_NOPE_DIM + D_V_DIM)
        return (q[..., :D_NOPE_DIM], q[..., D_NOPE_DIM:], kv[..., :D_NOPE_DIM],
                kv[..., D_NOPE_DIM:], kr[:, :, None, :])

    def assemble(qn, qr, kn, kr):
        q = jnp.concatenate([qn, qr], axis=-1)[:, :, :, None, :]
        k = jnp.concatenate([kn, jnp.broadcast_to(kr, kn.shape[:3] + (D_ROPE_DIM,))], axis=-1)
        return q, k
    qn_c, qr_c, kn_c, v_c, kr_c = expand(*pc)
    qn_l, qr_l, kn_l, v_l, kr_l = expand(*pl)
    qr_l = apply_rope_2d(qr_l, rope)
    kr_l = apply_rope_2d(kr_l, rope)
    q_c, k_c = assemble(qn_c, qr_c, kn_c, kr_c)
    q_l, k_l = assemble(qn_l, qr_l, kn_l, kr_l)
    scale = (D_NOPE_DIM + D_ROPE_DIM) ** -0.5

    def flat(o):
        return o.reshape(o.shape[:2] + (D_HEADS * D_V_DIM,))
    k_all = jnp.concatenate([k_c, k_l], axis=1)
    v_all = jnp.concatenate([v_c, v_l], axis=1)
    o_lat = sweep_query_blocks(lambda qb: gqa_core(qb, k_all, v_all, scale), q_l)
    o_ctx = flat(gqa_core(q_c, k_c, v_c, scale)) if need_ctx else None
    return o_ctx, flat(o_lat)


def trunk_layer(x_ctx, x_lat, mod_ctx, mod_lat, params, ropes, lam_init, need_ctx):
    (g_pre, g_post, w_ffn1_in, w_ffn1_out, w_ffn2_in, w_ffn2_out, w_in, w_out,
     lam_vecs, g_subln, g_qnorm, g_knorm, w_spatial, b_spatial, ln_g, ln_b,
     g_q_a, w_uq, g_kv_a, w_ukv) = params
    rope_a, rope_b, rope_d = ropes

    def mod(m, i):
        return m[:, i, None, :]

    def pre(xs, m, j):
        return rms_norm(xs, g_pre[j]) * (1.0 + mod(m, 3 * j + 1)) + mod(m, 3 * j)

    def post(xs, y, m, j, w):
        return xs + w * mod(m, 3 * j + 2) * rms_norm(y, g_post[j])

    x_ctx = post(x_ctx, swiglu(pre(x_ctx, mod_ctx, 0), w_ffn1_in, w_ffn1_out), mod_ctx, 0, FFN_RES)
    x_lat = post(x_lat, swiglu(pre(x_lat, mod_lat, 0), w_ffn1_in, w_ffn1_out), mod_lat, 0, FFN_RES)

    pc = split_cols(pre(x_ctx, mod_ctx, 1) @ w_in, IN_SIZES)
    pl = split_cols(pre(x_lat, mod_lat, 1) @ w_in, IN_SIZES)
    a_c, a_l = diff_attention_group(pc[0:3], pl[0:3], lam_vecs, g_subln, lam_init, rope_a, need_ctx)
    b_c, b_l = gqa_group(pc[3:6], pl[3:6], g_qnorm, g_knorm, rope_b, need_ctx)
    c_l = chunk_gmlp(pl[6], pl[7], w_spatial, b_spatial, ln_g, ln_b)
    d_c, d_l = mla_group(pc[8:11], pl[8:11], g_q_a, w_uq, g_kv_a, w_ukv, rope_d, need_ctx)
    y_lat = jnp.concatenate([a_l, b_l, c_l, d_l], axis=-1) @ w_out
    x_lat = post(x_lat, y_lat, mod_lat, 1, 1.0)

    x_lat = post(x_lat, swiglu(pre(x_lat, mod_lat, 2), w_ffn2_in, w_ffn2_out), mod_lat, 2, FFN_RES)

    if need_ctx:
        c_c = chunk_gmlp(pc[6], pc[7], w_spatial, b_spatial, ln_g, ln_b)
        y_ctx = jnp.concatenate([a_c, b_c, c_c, d_c], axis=-1) @ w_out
        x_ctx = post(x_ctx, y_ctx, mod_ctx, 1, 1.0)
        x_ctx = post(x_ctx, swiglu(pre(x_ctx, mod_ctx, 2), w_ffn2_in, w_ffn2_out), mod_ctx, 2, FFN_RES)
    return x_ctx, x_lat


def setup_inputs(seed: int = 0) -> dict:
    key = jax.random.key(seed)
    ks = jax.random.split(key, 26)
    f32 = jnp.float32
    L = DEPTH

    def nrm(k, shape, s):
        return jax.random.normal(k, shape, f32) * s

    def gain(k, shape):
        return 1.0 + 0.05 * jax.random.normal(k, shape, f32)
    return {
        'x': nrm(ks[0], (BATCH, SEQ, D_MODEL), 1.0),
        'c': nrm(ks[1], (BATCH, D_MODEL), 1.0),
        'ctx': nrm(ks[2], (BATCH, CTX_LEN, D_MODEL), 1.0),
        'c_ctx': nrm(ks[3], (D_MODEL,), 1.0),
        'w_ada': nrm(ks[4], (L, D_MODEL, N_MOD * D_MODEL), D_MODEL ** -0.5),
        'b_ada': nrm(ks[5], (L, N_MOD * D_MODEL), 0.02),
        'g_pre': gain(ks[6], (L, 3, D_MODEL)),
        'g_post': gain(ks[7], (L, 3, D_MODEL)),
        'w_ffn1_in': nrm(ks[8], (L, D_MODEL, 2 * D_FF), D_MODEL ** -0.5),
        'w_ffn1_out': nrm(ks[9], (L, D_FF, D_MODEL), D_FF ** -0.5),
        'w_ffn2_in': nrm(ks[10], (L, D_MODEL, 2 * D_FF), D_MODEL ** -0.5),
        'w_ffn2_out': nrm(ks[11], (L, D_FF, D_MODEL), D_FF ** -0.5),
        'w_in': nrm(ks[12], (L, D_MODEL, IN_WIDTH), D_MODEL ** -0.5),
        'w_out': nrm(ks[13], (L, MIX_WIDTH, D_MODEL), MIX_WIDTH ** -0.5),
        'lam_vecs': nrm(ks[14], (L, 4, A_QK_DIM), 0.1),
        'g_subln': gain(ks[15], (L, A_V_DIM)),
        'g_qnorm': gain(ks[16], (L, B_HEAD_DIM)),
        'g_knorm': gain(ks[17], (L, B_HEAD_DIM)),
        'w_spatial': nrm(ks[18], (L, C_GROUPS, CHUNK, CHUNK), CHUNK ** -0.5),
        'b_spatial': gain(ks[19], (L, C_GROUPS, CHUNK)),
        'ln_g': gain(ks[20], (L, GROUP_WIDTH)),
        'ln_b': nrm(ks[21], (L, GROUP_WIDTH), 0.02),
        'g_q_a': gain(ks[22], (L, D_Q_RANK)),
        'w_uq': nrm(ks[23], (L, D_Q_RANK, D_HEADS * (D_NOPE_DIM + D_ROPE_DIM)), D_Q_RANK ** -0.5),
        'g_kv_a': gain(ks[24], (L, D_KV_RANK)),
        'w_ukv': nrm(ks[25], (L, D_KV_RANK, D_HEADS * (D_NOPE_DIM + D_V_DIM)), D_KV_RANK ** -0.5),
    }


def reference(x, c, ctx, c_ctx, w_ada, b_ada, g_pre, g_post, w_ffn1_in, w_ffn1_out,
              w_ffn2_in, w_ffn2_out, w_in, w_out, lam_vecs, g_subln, g_qnorm, g_knorm,
              w_spatial, b_spatial, ln_g, ln_b, g_q_a, w_uq, g_kv_a, w_ukv):
    n_rows = x.shape[1] // GRID_W
    ropes = (axial_rope_tables(n_rows, A_QK_DIM),
             axial_rope_tables(n_rows, B_HEAD_DIM),
             axial_rope_tables(n_rows, D_ROPE_DIM))
    x_ctx, x_lat = ctx, x
    for l in range(DEPTH):
        mod_lat = (jax.nn.silu(c) @ w_ada[l] + b_ada[l]).reshape(c.shape[0], N_MOD, D_MODEL)
        mod_ctx = (jax.nn.silu(c_ctx)[None, :] @ w_ada[l] + b_ada[l]).reshape(1, N_MOD, D_MODEL)
        params = (g_pre[l], g_post[l], w_ffn1_in[l], w_ffn1_out[l], w_ffn2_in[l], w_ffn2_out[l],
                  w_in[l], w_out[l], lam_vecs[l], g_subln[l], g_qnorm[l], g_knorm[l],
                  w_spatial[l], b_spatial[l], ln_g[l], ln_b[l],
                  g_q_a[l], w_uq[l], g_kv_a[l], w_ukv[l])
        x_ctx, x_lat = trunk_layer(x_ctx, x_lat, mod_ctx, mod_lat, params, ropes,
                                   lambda_init(l), l < DEPTH - 1)
    return x_lat
```

```python
import functools
import math

import jax
import jax.numpy as jnp
import numpy as np
from jax import lax
from jax.experimental import pallas as pl
from jax.experimental.pallas import tpu as pltpu

F32 = jnp.float32
BF16 = jnp.bfloat16

GRID_W = 64
CHUNK = 128
ROPE_THETA = 10000.0
NORM_EPS = 1e-6
FFN_RES = 0.5
N_MOD = 9
N_MIXERS = 4
A_HEADS, B_HEADS, B_KV_HEADS, C_GROUPS, D_HEADS = 4, 4, 2, 4, 4

LANES = 128
MXU_DIM = 256
VMEM_LIMIT_BYTES = 58 * 2**20

ROW_TILE = 1024
FF_TILE = 256
KV_TILE = 256


def _dot(a, b):
    return jnp.dot(a, b, preferred_element_type=F32)


def _dot_nt(a, b):
    return lax.dot_general(a, b, (((1,), (1,)), ((), ())), preferred_element_type=F32)


def _split_bf16(x):
    hi = x.astype(BF16)
    lo = (x - hi.astype(F32)).astype(BF16)
    return hi, lo


def _group_mean(xsq, bd_ref):
    hi, lo = _split_bf16(xsq)
    bd = bd_ref[...]
    return _dot(hi, bd) + _dot(lo, bd)


def _rms(x):
    return x * lax.rsqrt(jnp.mean(x * x, axis=-1, keepdims=True) + NORM_EPS)


def _pre_norm(x, mod_ref, gpre_ref, j):
    g = gpre_ref[j:j + 1, :]
    shift = mod_ref[0, 3 * j:3 * j + 1, :]
    scale = mod_ref[0, 3 * j + 1:3 * j + 2, :]
    return (_rms(x) * g) * (1.0 + scale) + shift


def _post_norm(x, y, mod_ref, gpost_ref, j, res):
    g = gpost_ref[j:j + 1, :]
    gate = mod_ref[0, 3 * j + 2:3 * j + 3, :]
    return x + res * gate * (_rms(y) * g)


def _gelu_tanh(x):
    c = math.sqrt(2.0 / math.pi)
    return x * (0.5 * (1.0 + jnp.tanh(c * (x + 0.044715 * (x * x * x)))))


def _compiler_params(n_axes):
    return pltpu.CompilerParams(dimension_semantics=("arbitrary",) * n_axes,
                                vmem_limit_bytes=VMEM_LIMIT_BYTES)


def _resident(shape):
    nd = len(shape)
    return pl.BlockSpec(shape, lambda *_: (0,) * nd, pipeline_mode=pl.Buffered(1))


def _ada_kernel(c_ref, w_ref, b_ref, o_ref):
    c = c_ref[...]
    s = c * jax.nn.sigmoid(c)
    s_hi, s_lo = _split_bf16(s)
    w_hi, w_lo = _split_bf16(w_ref[0])
    o_ref[0] = _dot(s_hi, w_hi) + (_dot(s_hi, w_lo) + _dot(s_lo, w_hi)) + b_ref[0]


def _ada(cc, w_ada, b_ada):
    n_layers, d, n = w_ada.shape
    rows = cc.shape[0]
    tn = 1024
    return pl.pallas_call(
        _ada_kernel,
        grid=(n_layers, n // tn),
        in_specs=[pl.BlockSpec((rows, d), lambda l, j: (0, 0)),
                  pl.BlockSpec((1, d, tn), lambda l, j: (l, 0, j)),
                  pl.BlockSpec((1, 1, tn), lambda l, j: (l, 0, j))],
        out_specs=pl.BlockSpec((1, rows, tn), lambda l, j: (l, 0, j)),
        out_shape=jax.ShapeDtypeStruct((n_layers, rows, n), F32),
        compiler_params=_compiler_params(2),
        name="ada",
    )(cc, w_ada, b_ada.reshape(n_layers, 1, n))


def _ffn_kernel(x_ref, mod_ref, gpre_ref, gpost_ref, w1g_ref, w1u_ref, w2_ref, o_ref,
                xn_scr, acc_scr, *, j):
    x = x_ref[...]
    xn_scr[...] = _pre_norm(x, mod_ref, gpre_ref, j).astype(BF16)
    acc_scr[...] = jnp.zeros(acc_scr.shape, F32)

    def body(c, carry):
        xn = xn_scr[...]
        hg = _dot(xn, w1g_ref[c])
        hu = _dot(xn, w1u_ref[c])
        act = ((hg * jax.nn.sigmoid(hg)) * hu).astype(BF16)
        acc_scr[...] += _dot(act, w2_ref[c])
        return carry

    lax.fori_loop(0, w1g_ref.shape[0], body, 0)
    o_ref[...] = _post_norm(x, acc_scr[...], mod_ref, gpost_ref, j, FFN_RES)


def _ffn(x, mod, g_pre, g_post, w1g, w1u, w2, *, j, rows_per_mod):
    rows, d = x.shape
    tm = min(ROW_TILE, rows_per_mod)
    per_mod = rows_per_mod // tm
    return pl.pallas_call(
        functools.partial(_ffn_kernel, j=j),
        grid=(rows // tm,),
        in_specs=[pl.BlockSpec((tm, d), lambda i: (i, 0)),
                  pl.BlockSpec((1, N_MOD, d), lambda i: (i // per_mod, 0, 0)),
                  _resident(g_pre.shape), _resident(g_post.shape),
                  _resident(w1g.shape), _resident(w1u.shape), _resident(w2.shape)],
        out_specs=pl.BlockSpec((tm, d), lambda i: (i, 0)),
        out_shape=jax.ShapeDtypeStruct((rows, d), F32),
        scratch_shapes=[pltpu.VMEM((tm, d), BF16), pltpu.VMEM((tm, d), F32)],
        compiler_params=_compiler_params(1),
        name=f"ffn{j}",
    )(x, mod, g_pre, g_post, w1g, w1u, w2)


_IN_BLOCKS = ("aq", "ak", "av", "bq", "bk", "bv", "cu", "cv", "dq", "dkr", "dkv")
_IN_WIDTHS = dict(aq=256, ak=256, av=256, bq=256, bk=256, bv=256, cu=256, cv=256,
                  dq=256, dkr=512, dkv=128)
_IN_OFFSETS = {}
_off = 0
for _n in _IN_BLOCKS:
    _IN_OFFSETS[_n] = _off
    _off += _IN_WIDTHS[_n]
IN_EXT_WIDTH = _off


def _rope_slabs(x, tabs, shift, store):
    cos, s_up, s_dn = tabs
    for s in range(x.shape[1] // LANES):
        xs = x[:, s * LANES:(s + 1) * LANES]
        if cos is not None:
            xs = (xs * cos + pltpu.roll(xs, LANES - shift, axis=1) * s_up
                  + pltpu.roll(xs, shift, axis=1) * s_dn)
        store(s, xs.astype(BF16))


def _mix_in_kernel(*refs, rope):
    (x_ref, mod_ref, gpre_ref, win_ref, gq_ref, gk_ref, bd64_ref, lng_ref, lnb_ref, wsp_ref,
     bsp_ref, gqa_ref, wuq_ref, gkva_ref, wukvk_ref, wukvv_ref) = refs[:16]
    refs = refs[16:]
    if rope:
        tabs = [r[...] for r in refs[:9]]
        t32, t64, td = tabs[0:3], tabs[3:6], tabs[6:9]
        refs = refs[9:]
    else:
        t32 = t64 = td = (None, None, None)
    qa_ref, ka_ref, va_ref, qb_ref, kb_ref, vb_ref, cc_ref, qd_ref, kd_ref, vd_ref = refs

    xn = _pre_norm(x_ref[...], mod_ref, gpre_ref, 1).astype(BF16)

    def proj(name):
        o = _IN_OFFSETS[name]
        return _dot(xn, win_ref[:, o:o + _IN_WIDTHS[name]])

    def slab_store(ref):
        def store(s, v):
            ref[:, s * LANES:(s + 1) * LANES] = v
        return store

    _rope_slabs(proj("aq"), t32, 8, slab_store(qa_ref))
    _rope_slabs(proj("ak"), t32, 8, slab_store(ka_ref))
    va_ref[...] = proj("av").astype(BF16)

    def qk_norm(p, g_ref):
        return p * lax.rsqrt(_group_mean(p * p, bd64_ref) + NORM_EPS) * g_ref[...]

    _rope_slabs(qk_norm(proj("bq"), gq_ref), t64, 16, slab_store(qb_ref))
    _rope_slabs(qk_norm(proj("bk"), gk_ref), t64, 16, slab_store(kb_ref))
    vb_ref[...] = proj("bv").astype(BF16)

    u = _gelu_tanh(proj("cu"))
    v = _gelu_tanh(proj("cv"))
    mu = jnp.mean(v, axis=-1, keepdims=True)
    vc = v - mu
    vln = (vc * lax.rsqrt(jnp.mean(vc * vc, axis=-1, keepdims=True) + NORM_EPS) * lng_ref[...]
           + lnb_ref[...]).astype(BF16)
    width = vln.shape[1]
    gw = width // C_GROUPS
    lane = lax.broadcasted_iota(jnp.int32, (CHUNK, width), 1)
    wsp = wsp_ref[...]
    bsp = bsp_ref[...]
    for ci in range(vln.shape[0] // CHUNK):
        rows = slice(ci * CHUNK, (ci + 1) * CHUNK)
        allg = _dot(wsp, vln[rows, :])
        mixed = allg[0:CHUNK]
        for g in range(1, C_GROUPS):
            mixed = jnp.where(lane >= g * gw, allg[g * CHUNK:(g + 1) * CHUNK], mixed)
        cc_ref[rows, :] = (u[rows, :] * (mixed + bsp)).astype(BF16)

    cq = (_rms(proj("dq")) * gqa_ref[...]).astype(BF16)
    _rope_slabs(_dot(cq, wuq_ref[...]), td, 8, slab_store(qd_ref))
    ckv = (_rms(proj("dkv")) * gkva_ref[...]).astype(BF16)
    _rope_slabs(_dot(ckv, wukvk_ref[...]) + proj("dkr"), td, 8, slab_store(kd_ref))
    vd_ref[...] = _dot(ckv, wukvv_ref[...]).astype(BF16)


def _mix_in(x, mod, g_pre, params, tables, *, rows_per_mod, seq):
    rows, d = x.shape
    tm = min(ROW_TILE, rows_per_mod)
    per_mod = rows_per_mod // tm
    rope = tables is not None
    n_mod_blocks = rows // rows_per_mod
    if rope:
        grid = (seq // tm, n_mod_blocks)
        row_map = lambda t, b: (b * per_mod + t, 0)
        mod_map = lambda t, b: (b, 0, 0)
        tab_specs = [pl.BlockSpec((tm, LANES), lambda t, b: (t, 0)) for _ in tables]
    else:
        grid = (rows // tm, 1)
        row_map = lambda i, _: (i, 0)
        mod_map = lambda i, _: (i // per_mod, 0, 0)
        tab_specs = []
    out_widths = (256, 256, 256, 256, 256, 256, 256, 512, 512, 256)
    return pl.pallas_call(
        functools.partial(_mix_in_kernel, rope=rope),
        grid=grid,
        in_specs=[pl.BlockSpec((tm, d), row_map), pl.BlockSpec((1, N_MOD, d), mod_map),
                  _resident(g_pre.shape)] + [_resident(p.shape) for p in params] + tab_specs,
        out_specs=[pl.BlockSpec((tm, w), row_map) for w in out_widths],
        out_shape=[jax.ShapeDtypeStruct((rows, w), BF16) for w in out_widths],
        compiler_params=_compiler_params(2),
        name="mix_in_rope" if rope else "mix_in",
    )(x, mod, g_pre, *params, *(tables or ()))


_ATTN_CFG = dict(
    a=dict(groups=1, maps=8, map_w=32, pair=True, rows=1024),
    b=dict(groups=1, maps=4, map_w=64, pair=False, rows=1024),
    d=dict(groups=2, maps=2, map_w=128, pair=False, rows=512),
)


def _attn_kernel(*refs, cfg, n_src, tq, lam_init):
    q_ref = refs[0]
    src = refs[1:1 + 2 * n_src]
    refs = refs[1 + 2 * n_src:]
    if cfg["pair"]:
        lam_ref, gsub_ref, bd64_ref = refs[:3]
        refs = refs[3:]
    o_ref, k_scr, v_scr, s_scr, m_scr, l_scr, acc_scr = refs

    off = 0
    for i in range(n_src):
        n = src[2 * i].shape[0]
        k_scr[off:off + n, :] = src[2 * i][...]
        v_scr[off:off + n, :] = src[2 * i + 1][...]
        off += n
    n_keys = off
    n_chunks = n_keys // KV_TILE

    maps, map_w = cfg["maps"], cfg["map_w"]
    out_w = o_ref.shape[1]
    head_w = out_w // 4
    lane_q = lax.broadcasted_iota(jnp.int32, (tq, MXU_DIM), 1)
    lane_o = lax.broadcasted_iota(jnp.int32, (tq, out_w), 1)

    if cfg["pair"]:
        lv = lam_ref[...]
        lam = (jnp.exp(jnp.sum(lv[0:1] * lv[1:2], axis=-1, keepdims=True))
               - jnp.exp(jnp.sum(lv[2:3] * lv[3:4], axis=-1, keepdims=True)) + lam_init)

    def q_tile(i, carry):
        r0 = pl.multiple_of(i * tq, tq)
        q = q_ref[pl.ds(r0, tq), :]
        out = jnp.zeros((tq, out_w), F32)
        for g in range(cfg["groups"]):
            qg = q[:, g * MXU_DIM:(g + 1) * MXU_DIM]
            zero = jnp.zeros_like(qg)
            qs = jnp.concatenate(
                [jnp.where((lane_q >= m * map_w) & (lane_q < (m + 1) * map_w), qg, zero)
                 for m in range(maps)], axis=0)

            m_scr[...] = jnp.full(m_scr.shape, float(jnp.finfo(F32).min), F32)

            def pass1(c, carry):
                k0 = pl.multiple_of(c * KV_TILE, KV_TILE)
                s = _dot_nt(qs, k_scr[pl.ds(k0, KV_TILE), g * MXU_DIM:(g + 1) * MXU_DIM])
                s_scr[c] = s
                m_scr[...] = jnp.maximum(m_scr[...], jnp.maximum(s[:, :LANES], s[:, LANES:]))
                return carry

            lax.fori_loop(0, n_chunks, pass1, 0)
            m = jnp.max(m_scr[...], axis=-1, keepdims=True)
            l_scr[...] = jnp.zeros(l_scr.shape, F32)
            acc_scr[...] = jnp.zeros(acc_scr.shape, F32)

            def pass2(c, carry):
                k0 = pl.multiple_of(c * KV_TILE, KV_TILE)
                p = jnp.exp(s_scr[c] - m)
                l_scr[...] += p[:, :LANES] + p[:, LANES:]
                acc_scr[...] += _dot(p.astype(BF16), v_scr[pl.ds(k0, KV_TILE), :])
                return carry

            lax.fori_loop(0, n_chunks, pass2, 0)
            o = acc_scr[...] / jnp.sum(l_scr[...], axis=-1, keepdims=True)

            for mi in range(maps):
                om = o[mi * tq:(mi + 1) * tq]
                if cfg["pair"]:
                    h, second = mi // 2, mi % 2
                    if second:
                        continue
                    om = om - lam * o[(mi + 1) * tq:(mi + 2) * tq]
                else:
                    h = g * maps + mi
                out = jnp.where((lane_o >= h * head_w) & (lane_o < (h + 1) * head_w), om, out)
        if cfg["pair"]:
            out = (out * lax.rsqrt(_group_mean(out * out, bd64_ref) + NORM_EPS) * gsub_ref[...]
                   * (1.0 - lam_init))
        o_ref[pl.ds(r0, tq), :] = out.astype(BF16)
        return carry

    lax.fori_loop(0, q_ref.shape[0] // tq, q_tile, 0)


def _attn(kind, q, sources, extra, *, n_batch, lam_init=0.0):
    cfg = _ATTN_CFG[kind]
    rows, qw = q.shape
    sq = rows // n_batch
    tq = min(cfg["rows"] // cfg["maps"], sq)
    stacked = cfg["maps"] * tq
    n_keys = sum(k.shape[0] // n_batch for k, _ in sources)
    out_w = sources[0][1].shape[1]
    in_specs = [pl.BlockSpec((sq, qw), lambda b: (b, 0))]
    args = [q]
    for k, v in sources:
        n = k.shape[0] // n_batch
        in_specs += [pl.BlockSpec((n, k.shape[1]), lambda b: (b, 0)),
                     pl.BlockSpec((n, v.shape[1]), lambda b: (b, 0))]
        args += [k, v]
    if cfg["pair"]:
        in_specs += [_resident(e.shape) for e in extra]
        args += list(extra)
    return pl.pallas_call(
        functools.partial(_attn_kernel, cfg=cfg, n_src=len(sources), tq=tq, lam_init=lam_init),
        grid=(n_batch,),
        in_specs=in_specs,
        out_specs=pl.BlockSpec((sq, out_w), lambda b: (b, 0)),
        out_shape=jax.ShapeDtypeStruct((rows, out_w), BF16),
        scratch_shapes=[pltpu.VMEM((n_keys, qw), BF16), pltpu.VMEM((n_keys, out_w), BF16),
                        pltpu.VMEM((n_keys // KV_TILE, stacked, KV_TILE), F32),
                        pltpu.VMEM((stacked, LANES), F32), pltpu.VMEM((stacked, LANES), F32),
                        pltpu.VMEM((stacked, out_w), F32)],
        compiler_params=_compiler_params(1),
        name=f"attn_{kind}_{len(sources)}",
    )(*args)


def _mix_out_kernel(x_ref, a_ref, b_ref, c_ref, d_ref, wout_ref, mod_ref, gpost_ref, o_ref):
    y = _dot(a_ref[...], wout_ref[0])
    y += _dot(b_ref[...], wout_ref[1])
    y += _dot(c_ref[...], wout_ref[2])
    y += _dot(d_ref[...], wout_ref[3])
    o_ref[...] = _post_norm(x_ref[...], y, mod_ref, gpost_ref, 1, 1.0)


def _mix_out(x, groups, w_out, mod, g_post, *, rows_per_mod):
    rows, d = x.shape
    tm = min(ROW_TILE, rows_per_mod)
    per_mod = rows_per_mod // tm
    gw = groups[0].shape[1]
    return pl.pallas_call(
        _mix_out_kernel,
        grid=(rows // tm,),
        in_specs=[pl.BlockSpec((tm, d), lambda i: (i, 0))]
                 + [pl.BlockSpec((tm, gw), lambda i: (i, 0)) for _ in groups]
                 + [_resident(w_out.shape),
                    pl.BlockSpec((1, N_MOD, d), lambda i: (i // per_mod, 0, 0)),
                    _resident(g_post.shape)],
        out_specs=pl.BlockSpec((tm, d), lambda i: (i, 0)),
        out_shape=jax.ShapeDtypeStruct((rows, d), F32),
        compiler_params=_compiler_params(1),
        name="mix_out",
    )(x, *groups, w_out, mod, g_post)


def _rope_table(seq, pattern):
    t = jnp.arange(seq, dtype=jnp.int32)
    pos = (jnp.floor_divide(t, GRID_W).astype(F32), jnp.remainder(t, GRID_W).astype(F32))
    n_freq = next(e[2] for e in pattern if e is not None)
    axis_dim = 2 * n_freq
    inv_freq = ROPE_THETA ** (-jnp.arange(0, axis_dim, 2, dtype=F32) / axis_dim)
    active = np.array([e is not None for e in pattern])
    axis = np.array([e[0] if e is not None else 0 for e in pattern])
    freq = np.array([e[1] if e is not None else 0 for e in pattern])
    first = np.array([bool(e[3]) if e is not None else False for e in pattern])
    ang = jnp.where(axis[None, :] == 0, pos[0][:, None], pos[1][:, None]) * inv_freq[freq][None, :]
    cos = jnp.where(active[None, :], jnp.cos(ang), 1.0)
    sin = jnp.where(active[None, :], jnp.sin(ang), 0.0)
    return cos, jnp.where(first[None, :], -sin, 0.0), jnp.where(first[None, :], 0.0, sin)


def _rope_pattern(rot_dim):
    n_freq = rot_dim // 4
    return [(r // (2 * n_freq), r % n_freq, n_freq, (r % (2 * n_freq)) < n_freq)
            for r in range(rot_dim)]


def _rope_tables(seq):
    p32 = _rope_pattern(32) * (LANES // 32)
    p64 = _rope_pattern(64) * (LANES // 64)
    pd = [None] * 64 + _rope_pattern(32) + [None] * 32
    return _rope_table(seq, p32) + _rope_table(seq, p64) + _rope_table(seq, pd)


def _layer_params(l, w_in, g_qnorm, g_knorm, w_spatial, b_spatial, ln_g, ln_b, g_q_a, w_uq,
                  g_kv_a, w_ukv):
    d_model = w_in.shape[1]
    sizes = (256, 256, 256, 256, 128, 128, 256, 256, 256, 128, 32)
    offs = [0]
    for s in sizes:
        offs.append(offs[-1] + s)
    aq, ak, av, bq, bk, bv, cu, cv, dq, dkv, dkr = (w_in[l][:, offs[i]:offs[i + 1]]
                                                    for i in range(11))
    a_scale = 32 ** -0.5
    b_scale = 64 ** -0.5
    d_scale = 96 ** -0.5

    def rep_kv(w):
        hd = w.shape[1] // B_KV_HEADS
        n_rep = B_HEADS // B_KV_HEADS
        return jnp.concatenate([w[:, (h // n_rep) * hd:(h // n_rep + 1) * hd]
                                for h in range(B_HEADS)], axis=1)

    zeros = lambda n: jnp.zeros((d_model, n), w_in.dtype)
    dkr_placed = jnp.concatenate([jnp.concatenate([zeros(64), dkr, zeros(32)], axis=1)
                                  for _ in range(D_HEADS)], axis=1)
    blocks = dict(aq=aq * a_scale, ak=ak, av=av, bq=bq, bk=rep_kv(bk), bv=rep_kv(bv), cu=cu, cv=cv,
                  dq=dq, dkr=dkr_placed, dkv=dkv)
    w_ext = jnp.concatenate([blocks[n] for n in _IN_BLOCKS], axis=1).astype(BF16)

    def tile_row(v, n, scale=1.0):
        return (jnp.tile(v, n) * scale).reshape(1, -1)

    bd64 = jnp.kron(jnp.eye(4, dtype=F32), jnp.full((64, 64), 1.0 / 64, F32)).astype(BF16)
    wsp = w_spatial[l].reshape(C_GROUPS * CHUNK, CHUNK).astype(BF16)
    bsp = jnp.repeat(b_spatial[l].T, 256 // C_GROUPS, axis=1)

    uq = w_uq[l].reshape(-1, D_HEADS, 96) * d_scale
    uq = jnp.concatenate([uq, jnp.zeros(uq.shape[:2] + (32,), uq.dtype)], axis=-1)
    wuq = uq.reshape(uq.shape[0], D_HEADS * LANES).astype(BF16)
    ukv = w_ukv[l].reshape(-1, D_HEADS, 128)
    ukv_k = jnp.concatenate([ukv[..., :64], jnp.zeros(ukv.shape[:2] + (64,), ukv.dtype)], axis=-1)
    wukvk = ukv_k.reshape(ukv.shape[0], D_HEADS * LANES).astype(BF16)
    wukvv = ukv[..., 64:].reshape(ukv.shape[0], D_HEADS * 64).astype(BF16)

    return (w_ext, tile_row(g_qnorm[l], B_HEADS, b_scale), tile_row(g_knorm[l], B_HEADS), bd64,
            ln_g[l].reshape(1, -1), ln_b[l].reshape(1, -1), wsp, bsp,
            g_q_a[l].reshape(1, -1), wuq, g_kv_a[l].reshape(1, -1), wukvk, wukvv), bd64


def _ffn_weights(w_in, w_out):
    d, two_ff = w_in.shape
    d_ff = two_ff // 2
    n = d_ff // FF_TILE
    w1g = w_in[:, :d_ff].reshape(d, n, FF_TILE).transpose(1, 0, 2).astype(BF16)
    w1u = w_in[:, d_ff:].reshape(d, n, FF_TILE).transpose(1, 0, 2).astype(BF16)
    w2 = w_out.reshape(n, FF_TILE, d).astype(BF16)
    return w1g, w1u, w2


def _lambda_init(layer_idx):
    return 0.8 - 0.6 * math.exp(-0.3 * layer_idx)


def kernel(x, c, ctx, c_ctx, w_ada, b_ada, g_pre, g_post, w_ffn1_in, w_ffn1_out, w_ffn2_in,
           w_ffn2_out, w_in, w_out, lam_vecs, g_subln, g_qnorm, g_knorm, w_spatial, b_spatial,
           ln_g, ln_b, g_q_a, w_uq, g_kv_a, w_ukv):
    n_batch, seq, d = x.shape
    n_ctx = ctx.shape[1]
    depth = w_ada.shape[0]

    n_cond = n_batch + 1
    pad = (-n_cond) % 8
    cc = jnp.concatenate([c, c_ctx[None, :], jnp.zeros((pad, d), c.dtype)], axis=0)
    mod = _ada(cc, w_ada, b_ada)
    mod_lat = mod[:, :n_batch].reshape(depth, n_batch, N_MOD, d)
    mod_ctx = mod[:, n_batch:n_cond].reshape(depth, 1, N_MOD, d)

    tables = _rope_tables(seq)
    x_lat = x.reshape(n_batch * seq, d)
    x_ctx = ctx.reshape(n_batch * n_ctx, d)
    lat = dict(rows_per_mod=seq)
    ctxk = dict(rows_per_mod=n_batch * n_ctx)

    for l in range(depth):
        need_ctx = l < depth - 1
        ffn1 = _ffn_weights(w_ffn1_in[l], w_ffn1_out[l])
        ffn2 = _ffn_weights(w_ffn2_in[l], w_ffn2_out[l])
        params, bd64 = _layer_params(l, w_in, g_qnorm, g_knorm, w_spatial, b_spatial, ln_g, ln_b,
                                     g_q_a, w_uq, g_kv_a, w_ukv)
        wo = w_out[l].reshape(N_MIXERS, -1, d).astype(BF16)
        a_extra = (lam_vecs[l], jnp.tile(g_subln[l], A_HEADS).reshape(1, -1), bd64)
        lam0 = _lambda_init(l)

        x_ctx = _ffn(x_ctx, mod_ctx[l], g_pre[l], g_post[l], *ffn1, j=0, **ctxk)
        x_lat = _ffn(x_lat, mod_lat[l], g_pre[l], g_post[l], *ffn1, j=0, **lat)

        pc = _mix_in(x_ctx, mod_ctx[l], g_pre[l], params, None, seq=n_ctx, **ctxk)
        pl_ = _mix_in(x_lat, mod_lat[l], g_pre[l], params, tables, seq=seq, **lat)
        qa_c, ka_c, va_c, qb_c, kb_c, vb_c, cc_c, qd_c, kd_c, vd_c = pc
        qa_l, ka_l, va_l, qb_l, kb_l, vb_l, cc_l, qd_l, kd_l, vd_l = pl_

        a_l = _attn("a", qa_l, [(ka_l, va_l), (ka_c, va_c)], a_extra, n_batch=n_batch, lam_init=lam0)
        b_l = _attn("b", qb_l, [(kb_l, vb_l), (kb_c, vb_c)], (), n_batch=n_batch)
        d_l = _attn("d", qd_l, [(kd_l, vd_l), (kd_c, vd_c)], (), n_batch=n_batch)
        x_lat = _mix_out(x_lat, (a_l, b_l, cc_l, d_l), wo, mod_lat[l], g_post[l], **lat)
        x_lat = _ffn(x_lat, mod_lat[l], g_pre[l], g_post[l], *ffn2, j=2, **lat)

        if need_ctx:
            a_c = _attn("a", qa_c, [(ka_c, va_c)], a_extra, n_batch=n_batch, lam_init=lam0)
            b_c = _attn("b", qb_c, [(kb_c, vb_c)], (), n_batch=n_batch)
            d_c = _attn("d", qd_c, [(kd_c, vd_c)], (), n_batch=n_batch)
            x_ctx = _mix_out(x_ctx, (a_c, b_c, cc_c, d_c), wo, mod_ctx[l], g_post[l], **ctxk)
            x_ctx = _ffn(x_ctx, mod_ctx[l], g_pre[l], g_post[l], *ffn2, j=2, **ctxk)

    return x_lat.reshape(n_batch, seq, d)
```

```python
import functools
import math

import jax
import jax.numpy as jnp
import numpy as np
from jax import lax
from jax.experimental import pallas as pl
from jax.experimental.pallas import tpu as pltpu

F32 = jnp.float32
BF16 = jnp.bfloat16

GRID_W = 64
CHUNK = 128
ROPE_THETA = 10000.0
NORM_EPS = 1e-6
FFN_RES = 0.5
N_MOD = 9
N_MIXERS = 4
A_HEADS, B_HEADS, B_KV_HEADS, C_GROUPS, D_HEADS = 4, 4, 2, 4, 4

LANES = 128
MXU_DIM = 256
VMEM_LIMIT_BYTES = 58 * 2**20

ROW_TILE = 1024
FF_TILE = 256


def _dot(a, b):
    return jnp.dot(a, b, preferred_element_type=F32)


def _dot_nt(a, b):
    return lax.dot_general(a, b, (((1,), (1,)), ((), ())), preferred_element_type=F32)


def _split_bf16(x):
    hi = x.astype(BF16)
    lo = (x - hi.astype(F32)).astype(BF16)
    return hi, lo


def _group_mean(xsq, bd_ref):
    hi, lo = _split_bf16(xsq)
    bd = bd_ref[...]
    return _dot(hi, bd) + _dot(lo, bd)


def _rms(x):
    return x * lax.rsqrt(jnp.mean(x * x, axis=-1, keepdims=True) + NORM_EPS)


def _pre_norm(x, mod_ref, gpre_ref, j):
    shift = mod_ref[0, 3 * j:3 * j + 1, :]
    gain = gpre_ref[j:j + 1, :] * (1.0 + mod_ref[0, 3 * j + 1:3 * j + 2, :])
    return _rms(x) * gain + shift


def _post_norm(x, y, mod_ref, gpost_ref, j, res):
    gain = (res * mod_ref[0, 3 * j + 2:3 * j + 3, :]) * gpost_ref[j:j + 1, :]
    return x + _rms(y) * gain


def _gelu_tanh(x):
    c = math.sqrt(2.0 / math.pi)
    return x * (0.5 * (1.0 + jnp.tanh(c * (x + 0.044715 * (x * x * x)))))


def _compiler_params(n_axes):
    return pltpu.CompilerParams(dimension_semantics=("arbitrary",) * n_axes,
                                vmem_limit_bytes=VMEM_LIMIT_BYTES)


def _resident(shape):
    nd = len(shape)
    return pl.BlockSpec(shape, lambda *_: (0,) * nd, pipeline_mode=pl.Buffered(1))


def _ada_kernel(c_ref, w_ref, b_ref, o_ref):
    c = c_ref[...]
    s = c * jax.nn.sigmoid(c)
    s_hi, s_lo = _split_bf16(s)
    w_hi, w_lo = _split_bf16(w_ref[0])
    o_ref[0] = _dot(s_hi, w_hi) + (_dot(s_hi, w_lo) + _dot(s_lo, w_hi)) + b_ref[0]


def _ada(cc, w_ada, b_ada):
    n_layers, d, n = w_ada.shape
    rows = cc.shape[0]
    tn = 1024
    return pl.pallas_call(
        _ada_kernel,
        grid=(n_layers, n // tn),
        in_specs=[pl.BlockSpec((rows, d), lambda l, j: (0, 0)),
                  pl.BlockSpec((1, d, tn), lambda l, j: (l, 0, j)),
                  pl.BlockSpec((1, 1, tn), lambda l, j: (l, 0, j))],
        out_specs=pl.BlockSpec((1, rows, tn), lambda l, j: (l, 0, j)),
        out_shape=jax.ShapeDtypeStruct((n_layers, rows, n), F32),
        compiler_params=_compiler_params(2),
        name="ada",
    )(cc, w_ada, b_ada.reshape(n_layers, 1, n))


def _ffn_kernel(x_ref, mod_ref, gpre_ref, gpost_ref, w1g_ref, w1u_ref, w2_ref, o_ref,
                xn_scr, acc_scr, *, j):
    x = x_ref[...]
    xn_scr[...] = _pre_norm(x, mod_ref, gpre_ref, j).astype(BF16)
    acc_scr[...] = jnp.zeros(acc_scr.shape, F32)

    def body(c, carry):
        xn = xn_scr[...]
        hg = _dot(xn, w1g_ref[c])
        hu = _dot(xn, w1u_ref[c])
        act = ((hg * jax.nn.sigmoid(hg)) * hu).astype(BF16)
        acc_scr[...] += _dot(act, w2_ref[c])
        return carry

    lax.fori_loop(0, w1g_ref.shape[0], body, 0)
    o_ref[...] = _post_norm(x, acc_scr[...], mod_ref, gpost_ref, j, FFN_RES)


def _ffn(x, mod, g_pre, g_post, w1g, w1u, w2, *, j, rows_per_mod):
    rows, d = x.shape
    tm = min(ROW_TILE, rows_per_mod)
    per_mod = rows_per_mod // tm
    return pl.pallas_call(
        functools.partial(_ffn_kernel, j=j),
        grid=(rows // tm,),
        in_specs=[pl.BlockSpec((tm, d), lambda i: (i, 0)),
                  pl.BlockSpec((1, N_MOD, d), lambda i: (i // per_mod, 0, 0)),
                  _resident(g_pre.shape), _resident(g_post.shape),
                  _resident(w1g.shape), _resident(w1u.shape), _resident(w2.shape)],
        out_specs=pl.BlockSpec((tm, d), lambda i: (i, 0)),
        out_shape=jax.ShapeDtypeStruct((rows, d), F32),
        scratch_shapes=[pltpu.VMEM((tm, d), BF16), pltpu.VMEM((tm, d), F32)],
        compiler_params=_compiler_params(1),
        name=f"ffn{j}",
    )(x, mod, g_pre, g_post, w1g, w1u, w2)


_IN_BLOCKS = ("aq", "ak", "av", "bq", "bk", "bv", "cu", "cv", "dq", "dkr", "dkv")
_IN_WIDTHS = dict(aq=256, ak=256, av=256, bq=256, bk=256, bv=256, cu=256, cv=256,
                  dq=256, dkr=512, dkv=128)
_IN_OFFSETS = {}
_off = 0
for _n in _IN_BLOCKS:
    _IN_OFFSETS[_n] = _off
    _off += _IN_WIDTHS[_n]
IN_EXT_WIDTH = _off


def _rope_slabs(x, tabs, shift, store):
    cos, s_up, s_dn = tabs
    for s in range(x.shape[1] // LANES):
        xs = x[:, s * LANES:(s + 1) * LANES]
        if cos is not None:
            xs = (xs * cos + pltpu.roll(xs, LANES - shift, axis=1) * s_up
                  + pltpu.roll(xs, shift, axis=1) * s_dn)
        store(s, xs.astype(BF16))


def _mix_in_kernel(*refs, rope):
    (x_ref, mod_ref, gpre_ref, win_ref, gq_ref, gk_ref, bd64_ref, lng_ref, lnb_ref, wsp_ref,
     bsp_ref, gqa_ref, wuq_ref, gkva_ref, wukvk_ref, wukvv_ref) = refs[:16]
    refs = refs[16:]
    if rope:
        tabs = [r[...] for r in refs[:9]]
        t32, t64, td = tabs[0:3], tabs[3:6], tabs[6:9]
        refs = refs[9:]
    else:
        t32 = t64 = td = (None, None, None)
    qa_ref, ka_ref, va_ref, qb_ref, kb_ref, vb_ref, cc_ref, qd_ref, kd_ref, vd_ref = refs

    xn = _pre_norm(x_ref[...], mod_ref, gpre_ref, 1).astype(BF16)

    def proj(name):
        o = _IN_OFFSETS[name]
        return _dot(xn, win_ref[:, o:o + _IN_WIDTHS[name]])

    def slab_store(ref):
        def store(s, v):
            ref[:, s * LANES:(s + 1) * LANES] = v
        return store

    _rope_slabs(proj("aq"), t32, 8, slab_store(qa_ref))
    _rope_slabs(proj("ak"), t32, 8, slab_store(ka_ref))
    va_ref[...] = proj("av").astype(BF16)

    def qk_norm(p, g_ref):
        return p * lax.rsqrt(_group_mean(p * p, bd64_ref) + NORM_EPS) * g_ref[...]

    _rope_slabs(qk_norm(proj("bq"), gq_ref), t64, 16, slab_store(qb_ref))
    _rope_slabs(qk_norm(proj("bk"), gk_ref), t64, 16, slab_store(kb_ref))
    vb_ref[...] = proj("bv").astype(BF16)

    u = _gelu_tanh(proj("cu"))
    v = _gelu_tanh(proj("cv"))
    mu = jnp.mean(v, axis=-1, keepdims=True)
    vc = v - mu
    vln = (vc * lax.rsqrt(jnp.mean(vc * vc, axis=-1, keepdims=True) + NORM_EPS) * lng_ref[...]
           + lnb_ref[...]).astype(BF16)
    width = vln.shape[1]
    gw = width // C_GROUPS
    lane = lax.broadcasted_iota(jnp.int32, (CHUNK, width), 1)
    wsp = wsp_ref[...]
    bsp = bsp_ref[...]
    for ci in range(vln.shape[0] // CHUNK):
        rows = slice(ci * CHUNK, (ci + 1) * CHUNK)
        allg = _dot(wsp, vln[rows, :])
        mixed = allg[0:CHUNK]
        for g in range(1, C_GROUPS):
            mixed = jnp.where(lane >= g * gw, allg[g * CHUNK:(g + 1) * CHUNK], mixed)
        cc_ref[rows, :] = (u[rows, :] * (mixed + bsp)).astype(BF16)

    cq = (_rms(proj("dq")) * gqa_ref[...]).astype(BF16)
    _rope_slabs(_dot(cq, wuq_ref[...]), td, 8, slab_store(qd_ref))
    ckv = (_rms(proj("dkv")) * gkva_ref[...]).astype(BF16)
    _rope_slabs(_dot(ckv, wukvk_ref[...]) + proj("dkr"), td, 8, slab_store(kd_ref))
    vd_ref[...] = _dot(ckv, wukvv_ref[...]).astype(BF16)


def _mix_in(x, mod, g_pre, params, tables, *, rows_per_mod, seq):
    rows, d = x.shape
    tm = min(ROW_TILE, rows_per_mod)
    per_mod = rows_per_mod // tm
    rope = tables is not None
    n_mod_blocks = rows // rows_per_mod
    if rope:
        grid = (seq // tm, n_mod_blocks)
        row_map = lambda t, b: (b * per_mod + t, 0)
        mod_map = lambda t, b: (b, 0, 0)
        tab_specs = [pl.BlockSpec((tm, LANES), lambda t, b: (t, 0)) for _ in tables]
    else:
        grid = (rows // tm, 1)
        row_map = lambda i, _: (i, 0)
        mod_map = lambda i, _: (i // per_mod, 0, 0)
        tab_specs = []
    out_widths = (256, 256, 256, 256, 256, 256, 256, 512, 512, 256)
    return pl.pallas_call(
        functools.partial(_mix_in_kernel, rope=rope),
        grid=grid,
        in_specs=[pl.BlockSpec((tm, d), row_map), pl.BlockSpec((1, N_MOD, d), mod_map),
                  _resident(g_pre.shape)] + [_resident(p.shape) for p in params] + tab_specs,
        out_specs=[pl.BlockSpec((tm, w), row_map) for w in out_widths],
        out_shape=[jax.ShapeDtypeStruct((rows, w), BF16) for w in out_widths],
        compiler_params=_compiler_params(2),
        name="mix_in_rope" if rope else "mix_in",
    )(x, mod, g_pre, *params, *(tables or ()))


_ATTN_CFG = dict(
    a=dict(groups=1, maps=8, map_w=32, pair=True, rows=1024, kv=1024),
    b=dict(groups=1, maps=4, map_w=64, pair=False, rows=1024, kv=1024),
    d=dict(groups=2, maps=2, map_w=128, pair=False, rows=512, kv=2048),
)


def _lane_fold(x, op):
    parts = [x[:, j * LANES:(j + 1) * LANES] for j in range(x.shape[1] // LANES)]
    while len(parts) > 1:
        parts = [op(parts[i], parts[i + 1]) for i in range(0, len(parts) - 1, 2)] + (
            [parts[-1]] if len(parts) % 2 else [])
    return parts[0]


def _attn_kernel(*refs, cfg, n_src, tq, kv, lam_init):
    q_ref = refs[0]
    src = refs[1:1 + 2 * n_src]
    refs = refs[1 + 2 * n_src:]
    if cfg["pair"]:
        lam_ref, gsub_ref, bd64_ref = refs[:3]
        refs = refs[3:]
    o_ref, k_scr, v_scr, s_big, s_rem, m_scr, l_scr, acc_scr = refs

    off = 0
    for i in range(n_src):
        n = src[2 * i].shape[0]
        k_scr[off:off + n, :] = src[2 * i][...]
        v_scr[off:off + n, :] = src[2 * i + 1][...]
        off += n
    n_keys = off
    n_big, rem = divmod(n_keys, kv)

    maps, map_w = cfg["maps"], cfg["map_w"]
    out_w = o_ref.shape[1]
    head_w = out_w // 4
    lane_q = lax.broadcasted_iota(jnp.int32, (tq, MXU_DIM), 1)
    lane_o = lax.broadcasted_iota(jnp.int32, (tq, out_w), 1)

    if cfg["pair"]:
        lv = lam_ref[...]
        lam = (jnp.exp(jnp.sum(lv[0:1] * lv[1:2], axis=-1, keepdims=True))
               - jnp.exp(jnp.sum(lv[2:3] * lv[3:4], axis=-1, keepdims=True)) + lam_init)

    def q_tile(i, carry):
        r0 = pl.multiple_of(i * tq, tq)
        q = q_ref[pl.ds(r0, tq), :]
        out = jnp.zeros((tq, out_w), F32)
        for g in range(cfg["groups"]):
            qg = q[:, g * MXU_DIM:(g + 1) * MXU_DIM]
            zero = jnp.zeros_like(qg)
            qs = jnp.concatenate(
                [jnp.where((lane_q >= m * map_w) & (lane_q < (m + 1) * map_w), qg, zero)
                 for m in range(maps)], axis=0)

            kcols = slice(g * MXU_DIM, (g + 1) * MXU_DIM)

            def scores(k_rows, s_dst):
                s = _dot_nt(qs, k_scr[k_rows, kcols])
                s_dst[...] = s
                m_scr[...] = jnp.maximum(m_scr[...], _lane_fold(s, jnp.maximum))

            m_scr[...] = jnp.full(m_scr.shape, float(jnp.finfo(F32).min), F32)

            def pass1(c, carry):
                scores(pl.ds(pl.multiple_of(c * kv, kv), kv), s_big.at[c])
                return carry

            if n_big:
                lax.fori_loop(0, n_big, pass1, 0)
            if rem:
                scores(slice(n_big * kv, n_keys), s_rem)
            m = jnp.max(m_scr[...], axis=-1, keepdims=True)

            def weigh(k_rows, s_src):
                p = jnp.exp2(s_src[...] - m)
                l_scr[...] += _lane_fold(p, jnp.add)
                acc_scr[...] += _dot(p.astype(BF16), v_scr[k_rows, :])

            l_scr[...] = jnp.zeros(l_scr.shape, F32)
            acc_scr[...] = jnp.zeros(acc_scr.shape, F32)

            def pass2(c, carry):
                weigh(pl.ds(pl.multiple_of(c * kv, kv), kv), s_big.at[c])
                return carry

            if n_big:
                lax.fori_loop(0, n_big, pass2, 0)
            if rem:
                weigh(slice(n_big * kv, n_keys), s_rem)
            o = acc_scr[...] / jnp.sum(l_scr[...], axis=-1, keepdims=True)


            for mi in range(maps):
                om = o[mi * tq:(mi + 1) * tq]
                if cfg["pair"]:
                    h, second = mi // 2, mi % 2
                    if second:
                        continue
                    om = om - lam * o[(mi + 1) * tq:(mi + 2) * tq]
                else:
                    h = g * maps + mi
                out = jnp.where((lane_o >= h * head_w) & (lane_o < (h + 1) * head_w), om, out)
        if cfg["pair"]:
            out = (out * lax.rsqrt(_group_mean(out * out, bd64_ref) + NORM_EPS) * gsub_ref[...]
                   * (1.0 - lam_init))
        o_ref[pl.ds(r0, tq), :] = out.astype(BF16)
        return carry

    lax.fori_loop(0, q_ref.shape[0] // tq, q_tile, 0)


def _attn(kind, q, sources, extra, *, n_batch, lam_init=0.0):
    cfg = _ATTN_CFG[kind]
    rows, qw = q.shape
    sq = rows // n_batch
    tq = min(cfg["rows"] // cfg["maps"], sq)
    stacked = cfg["maps"] * tq
    n_keys = sum(k.shape[0] // n_batch for k, _ in sources)
    out_w = sources[0][1].shape[1]
    in_specs = [pl.BlockSpec((sq, qw), lambda b: (b, 0))]
    args = [q]
    for k, v in sources:
        n = k.shape[0] // n_batch
        in_specs += [pl.BlockSpec((n, k.shape[1]), lambda b: (b, 0)),
                     pl.BlockSpec((n, v.shape[1]), lambda b: (b, 0))]
        args += [k, v]
    if cfg["pair"]:
        in_specs += [_resident(e.shape) for e in extra]
        args += list(extra)
    kv = cfg["kv"]
    n_big, rem = divmod(n_keys, kv)
    return pl.pallas_call(
        functools.partial(_attn_kernel, cfg=cfg, n_src=len(sources), tq=tq, kv=kv,
                          lam_init=lam_init),
        grid=(n_batch,),
        in_specs=in_specs,
        out_specs=pl.BlockSpec((sq, out_w), lambda b: (b, 0)),
        out_shape=jax.ShapeDtypeStruct((rows, out_w), BF16),
        scratch_shapes=[pltpu.VMEM((n_keys, qw), BF16), pltpu.VMEM((n_keys, out_w), BF16),
                        pltpu.VMEM((n_big, stacked, kv) if n_big else (1, 8, LANES), F32),
                        pltpu.VMEM((stacked, rem) if rem else (8, LANES), F32),
                        pltpu.VMEM((stacked, LANES), F32), pltpu.VMEM((stacked, LANES), F32),
                        pltpu.VMEM((stacked, out_w), F32)],
        compiler_params=_compiler_params(1),
        name=f"attn_{kind}_{len(sources)}",
    )(*args)


def _mix_out_kernel(x_ref, a_ref, b_ref, c_ref, d_ref, wout_ref, mod_ref, gpost_ref, o_ref):
    y = _dot(a_ref[...], wout_ref[0])
    y += _dot(b_ref[...], wout_ref[1])
    y += _dot(c_ref[...], wout_ref[2])
    y += _dot(d_ref[...], wout_ref[3])
    o_ref[...] = _post_norm(x_ref[...], y, mod_ref, gpost_ref, 1, 1.0)


def _mix_out(x, groups, w_out, mod, g_post, *, rows_per_mod):
    rows, d = x.shape
    tm = min(ROW_TILE, rows_per_mod)
    per_mod = rows_per_mod // tm
    gw = groups[0].shape[1]
    return pl.pallas_call(
        _mix_out_kernel,
        grid=(rows // tm,),
        in_specs=[pl.BlockSpec((tm, d), lambda i: (i, 0))]
                 + [pl.BlockSpec((tm, gw), lambda i: (i, 0)) for _ in groups]
                 + [_resident(w_out.shape),
                    pl.BlockSpec((1, N_MOD, d), lambda i: (i // per_mod, 0, 0)),
                    _resident(g_post.shape)],
        out_specs=pl.BlockSpec((tm, d), lambda i: (i, 0)),
        out_shape=jax.ShapeDtypeStruct((rows, d), F32),
        compiler_params=_compiler_params(1),
        name="mix_out",
    )(x, *groups, w_out, mod, g_post)


def _rope_table(seq, pattern):
    t = jnp.arange(seq, dtype=jnp.int32)
    pos = (jnp.floor_divide(t, GRID_W).astype(F32), jnp.remainder(t, GRID_W).astype(F32))
    n_freq = next(e[2] for e in pattern if e is not None)
    axis_dim = 2 * n_freq
    inv_freq = ROPE_THETA ** (-jnp.arange(0, axis_dim, 2, dtype=F32) / axis_dim)
    active = np.array([e is not None for e in pattern])
    axis = np.array([e[0] if e is not None else 0 for e in pattern])
    freq = np.array([e[1] if e is not None else 0 for e in pattern])
    first = np.array([bool(e[3]) if e is not None else False for e in pattern])
    ang = jnp.where(axis[None, :] == 0, pos[0][:, None], pos[1][:, None]) * inv_freq[freq][None, :]
    cos = jnp.where(active[None, :], jnp.cos(ang), 1.0)
    sin = jnp.where(active[None, :], jnp.sin(ang), 0.0)
    return cos, jnp.where(first[None, :], -sin, 0.0), jnp.where(first[None, :], 0.0, sin)


def _rope_pattern(rot_dim):
    n_freq = rot_dim // 4
    return [(r // (2 * n_freq), r % n_freq, n_freq, (r % (2 * n_freq)) < n_freq)
            for r in range(rot_dim)]


def _rope_tables(seq):
    p32 = _rope_pattern(32) * (LANES // 32)
    p64 = _rope_pattern(64) * (LANES // 64)
    pd = [None] * 64 + _rope_pattern(32) + [None] * 32
    return _rope_table(seq, p32) + _rope_table(seq, p64) + _rope_table(seq, pd)


def _layer_params(l, w_in, g_qnorm, g_knorm, w_spatial, b_spatial, ln_g, ln_b, g_q_a, w_uq,
                  g_kv_a, w_ukv):
    d_model = w_in.shape[1]
    sizes = (256, 256, 256, 256, 128, 128, 256, 256, 256, 128, 32)
    offs = [0]
    for s in sizes:
        offs.append(offs[-1] + s)
    aq, ak, av, bq, bk, bv, cu, cv, dq, dkv, dkr = (w_in[l][:, offs[i]:offs[i + 1]]
                                                    for i in range(11))
    log2e = math.log2(math.e)
    a_scale = 32 ** -0.5 * log2e
    b_scale = 64 ** -0.5 * log2e
    d_scale = 96 ** -0.5 * log2e

    def rep_kv(w):
        hd = w.shape[1] // B_KV_HEADS
        n_rep = B_HEADS // B_KV_HEADS
        return jnp.concatenate([w[:, (h // n_rep) * hd:(h // n_rep + 1) * hd]
                                for h in range(B_HEADS)], axis=1)

    zeros = lambda n: jnp.zeros((d_model, n), w_in.dtype)
    dkr_placed = jnp.concatenate([jnp.concatenate([zeros(64), dkr, zeros(32)], axis=1)
                                  for _ in range(D_HEADS)], axis=1)
    blocks = dict(aq=aq * a_scale, ak=ak, av=av, bq=bq, bk=rep_kv(bk), bv=rep_kv(bv), cu=cu, cv=cv,
                  dq=dq, dkr=dkr_placed, dkv=dkv)
    w_ext = jnp.concatenate([blocks[n] for n in _IN_BLOCKS], axis=1).astype(BF16)

    def tile_row(v, n, scale=1.0):
        return (jnp.tile(v, n) * scale).reshape(1, -1)

    bd64 = jnp.kron(jnp.eye(4, dtype=F32), jnp.full((64, 64), 1.0 / 64, F32)).astype(BF16)
    wsp = w_spatial[l].reshape(C_GROUPS * CHUNK, CHUNK).astype(BF16)
    bsp = jnp.repeat(b_spatial[l].T, 256 // C_GROUPS, axis=1)

    uq = w_uq[l].reshape(-1, D_HEADS, 96) * d_scale
    uq = jnp.concatenate([uq, jnp.zeros(uq.shape[:2] + (32,), uq.dtype)], axis=-1)
    wuq = uq.reshape(uq.shape[0], D_HEADS * LANES).astype(BF16)
    ukv = w_ukv[l].reshape(-1, D_HEADS, 128)
    ukv_k = jnp.concatenate([ukv[..., :64], jnp.zeros(ukv.shape[:2] + (64,), ukv.dtype)], axis=-1)
    wukvk = ukv_k.reshape(ukv.shape[0], D_HEADS * LANES).astype(BF16)
    wukvv = ukv[..., 64:].reshape(ukv.shape[0], D_HEADS * 64).astype(BF16)

    return (w_ext, tile_row(g_qnorm[l], B_HEADS, b_scale), tile_row(g_knorm[l], B_HEADS), bd64,
            ln_g[l].reshape(1, -1), ln_b[l].reshape(1, -1), wsp, bsp,
            g_q_a[l].reshape(1, -1), wuq, g_kv_a[l].reshape(1, -1), wukvk, wukvv), bd64


def _ffn_weights(w_in, w_out):
    d, two_ff = w_in.shape
    d_ff = two_ff // 2
    n = d_ff // FF_TILE
    w1g = w_in[:, :d_ff].reshape(d, n, FF_TILE).transpose(1, 0, 2).astype(BF16)
    w1u = w_in[:, d_ff:].reshape(d, n, FF_TILE).transpose(1, 0, 2).astype(BF16)
    w2 = w_out.reshape(n, FF_TILE, d).astype(BF16)
    return w1g, w1u, w2


def _lambda_init(layer_idx):
    return 0.8 - 0.6 * math.exp(-0.3 * layer_idx)


def kernel(x, c, ctx, c_ctx, w_ada, b_ada, g_pre, g_post, w_ffn1_in, w_ffn1_out, w_ffn2_in,
           w_ffn2_out, w_in, w_out, lam_vecs, g_subln, g_qnorm, g_knorm, w_spatial, b_spatial,
           ln_g, ln_b, g_q_a, w_uq, g_kv_a, w_ukv):
    n_batch, seq, d = x.shape
    n_ctx = ctx.shape[1]
    depth = w_ada.shape[0]

    n_cond = n_batch + 1
    pad = (-n_cond) % 8
    cc = jnp.concatenate([c, c_ctx[None, :], jnp.zeros((pad, d), c.dtype)], axis=0)
    mod = _ada(cc, w_ada, b_ada)
    mod_lat = mod[:, :n_batch].reshape(depth, n_batch, N_MOD, d)
    mod_ctx = mod[:, n_batch:n_cond].reshape(depth, 1, N_MOD, d)

    tables = _rope_tables(seq)
    x_lat = x.reshape(n_batch * seq, d)
    x_ctx = ctx.reshape(n_batch * n_ctx, d)
    lat = dict(rows_per_mod=seq)
    ctxk = dict(rows_per_mod=n_batch * n_ctx)

    for l in range(depth):
        need_ctx = l < depth - 1
        ffn1 = _ffn_weights(w_ffn1_in[l], w_ffn1_out[l])
        ffn2 = _ffn_weights(w_ffn2_in[l], w_ffn2_out[l])
        params, bd64 = _layer_params(l, w_in, g_qnorm, g_knorm, w_spatial, b_spatial, ln_g, ln_b,
                                     g_q_a, w_uq, g_kv_a, w_ukv)
        wo = w_out[l].reshape(N_MIXERS, -1, d).astype(BF16)
        a_extra = (lam_vecs[l], jnp.tile(g_subln[l], A_HEADS).reshape(1, -1), bd64)
        lam0 = _lambda_init(l)

        x_ctx = _ffn(x_ctx, mod_ctx[l], g_pre[l], g_post[l], *ffn1, j=0, **ctxk)
        x_lat = _ffn(x_lat, mod_lat[l], g_pre[l], g_post[l], *ffn1, j=0, **lat)

        pc = _mix_in(x_ctx, mod_ctx[l], g_pre[l], params, None, seq=n_ctx, **ctxk)
        pl_ = _mix_in(x_lat, mod_lat[l], g_pre[l], params, tables, seq=seq, **lat)
        qa_c, ka_c, va_c, qb_c, kb_c, vb_c, cc_c, qd_c, kd_c, vd_c = pc
        qa_l, ka_l, va_l, qb_l, kb_l, vb_l, cc_l, qd_l, kd_l, vd_l = pl_

        a_l = _attn("a", qa_l, [(ka_l, va_l), (ka_c, va_c)], a_extra, n_batch=n_batch, lam_init=lam0)
        b_l = _attn("b", qb_l, [(kb_l, vb_l), (kb_c, vb_c)], (), n_batch=n_batch)
        d_l = _attn("d", qd_l, [(kd_l, vd_l), (kd_c, vd_c)], (), n_batch=n_batch)
        x_lat = _mix_out(x_lat, (a_l, b_l, cc_l, d_l), wo, mod_lat[l], g_post[l], **lat)
        x_lat = _ffn(x_lat, mod_lat[l], g_pre[l], g_post[l], *ffn2, j=2, **lat)

        if need_ctx:
            a_c = _attn("a", qa_c, [(ka_c, va_c)], a_extra, n_batch=n_batch, lam_init=lam0)
            b_c = _attn("b", qb_c, [(kb_c, vb_c)], (), n_batch=n_batch)
            d_c = _attn("d", qd_c, [(kd_c, vd_c)], (), n_batch=n_batch)
            x_ctx = _mix_out(x_ctx, (a_c, b_c, cc_c, d_c), wo, mod_ctx[l], g_post[l], **ctxk)
            x_ctx = _ffn(x_ctx, mod_ctx[l], g_pre[l], g_post[l], *ffn2, j=2, **ctxk)

    return x_lat.reshape(n_batch, seq, d)
```

```python
import functools
import math

import jax
import jax.numpy as jnp
import numpy as np
from jax import lax
from jax.experimental import pallas as pl
from jax.experimental.pallas import tpu as pltpu

F32 = jnp.float32
BF16 = jnp.bfloat16

GRID_W = 64
CHUNK = 128
ROPE_THETA = 10000.0
NORM_EPS = 1e-6
FFN_RES = 0.5
N_MOD = 9
N_MIXERS = 4
A_HEADS, B_HEADS, B_KV_HEADS, C_GROUPS, D_HEADS = 4, 4, 2, 4, 4

LANES = 128
MXU_DIM = 256
VMEM_LIMIT_BYTES = 58 * 2**20

ROW_TILE = 1024
FF_TILE = 256


def _dot(a, b):
    return jnp.dot(a, b, preferred_element_type=F32)


def _dot_nt(a, b):
    return lax.dot_general(a, b, (((1,), (1,)), ((), ())), preferred_element_type=F32)


def _split_bf16(x):
    hi = x.astype(BF16)
    lo = (x - hi.astype(F32)).astype(BF16)
    return hi, lo


def _group_mean(xsq, bd_ref):
    hi, lo = _split_bf16(xsq)
    bd = bd_ref[...]
    return _dot(hi, bd) + _dot(lo, bd)


def _rms(x):
    return x * lax.rsqrt(jnp.mean(x * x, axis=-1, keepdims=True) + NORM_EPS)


def _pre_norm(x, mod_ref, gpre_ref, j):
    shift = mod_ref[0, 3 * j:3 * j + 1, :]
    gain = gpre_ref[j:j + 1, :] * (1.0 + mod_ref[0, 3 * j + 1:3 * j + 2, :])
    return _rms(x) * gain + shift


def _post_norm(x, y, mod_ref, gpost_ref, j, res):
    gain = (res * mod_ref[0, 3 * j + 2:3 * j + 3, :]) * gpost_ref[j:j + 1, :]
    return x + _rms(y) * gain


def _gelu_tanh(x):
    c = math.sqrt(2.0 / math.pi)
    return x * (0.5 * (1.0 + jnp.tanh(c * (x + 0.044715 * (x * x * x)))))


def _compiler_params(n_axes):
    return pltpu.CompilerParams(dimension_semantics=("arbitrary",) * n_axes,
                                vmem_limit_bytes=VMEM_LIMIT_BYTES)


def _resident(shape):
    nd = len(shape)
    return pl.BlockSpec(shape, lambda *_: (0,) * nd, pipeline_mode=pl.Buffered(1))


def _ada_kernel(c_ref, w_ref, b_ref, o_ref):
    c = c_ref[...]
    s = c * jax.nn.sigmoid(c)
    s_hi, s_lo = _split_bf16(s)
    w_hi, w_lo = _split_bf16(w_ref[0])
    o_ref[0] = _dot(s_hi, w_hi) + (_dot(s_hi, w_lo) + _dot(s_lo, w_hi)) + b_ref[0]


def _ada(cc, w_ada, b_ada):
    n_layers, d, n = w_ada.shape
    rows = cc.shape[0]
    tn = 1024
    return pl.pallas_call(
        _ada_kernel,
        grid=(n_layers, n // tn),
        in_specs=[pl.BlockSpec((rows, d), lambda l, j: (0, 0)),
                  pl.BlockSpec((1, d, tn), lambda l, j: (l, 0, j)),
                  pl.BlockSpec((1, 1, tn), lambda l, j: (l, 0, j))],
        out_specs=pl.BlockSpec((1, rows, tn), lambda l, j: (l, 0, j)),
        out_shape=jax.ShapeDtypeStruct((n_layers, rows, n), F32),
        compiler_params=_compiler_params(2),
        name="ada",
    )(cc, w_ada, b_ada.reshape(n_layers, 1, n))


def _ffn_kernel(x_ref, mod_ref, gpre_ref, gpost_ref, w1g_ref, w1u_ref, w2_ref, o_ref,
                xn_scr, acc_scr, *, j):
    x = x_ref[...]
    xn_scr[...] = _pre_norm(x, mod_ref, gpre_ref, j).astype(BF16)
    acc_scr[...] = jnp.zeros(acc_scr.shape, F32)

    def body(c, carry):
        xn = xn_scr[...]
        hg = _dot(xn, w1g_ref[c])
        hu = _dot(xn, w1u_ref[c])
        act = ((hg * jax.nn.sigmoid(hg)) * hu).astype(BF16)
        acc_scr[...] += _dot(act, w2_ref[c])
        return carry

    lax.fori_loop(0, w1g_ref.shape[0], body, 0)
    o_ref[...] = _post_norm(x, acc_scr[...], mod_ref, gpost_ref, j, FFN_RES)


def _ffn(x, mod, g_pre, g_post, w1g, w1u, w2, *, j, rows_per_mod):
    rows, d = x.shape
    tm = min(ROW_TILE, rows_per_mod)
    per_mod = rows_per_mod // tm
    return pl.pallas_call(
        functools.partial(_ffn_kernel, j=j),
        grid=(rows // tm,),
        in_specs=[pl.BlockSpec((tm, d), lambda i: (i, 0)),
                  pl.BlockSpec((1, N_MOD, d), lambda i: (i // per_mod, 0, 0)),
                  _resident(g_pre.shape), _resident(g_post.shape),
                  _resident(w1g.shape), _resident(w1u.shape), _resident(w2.shape)],
        out_specs=pl.BlockSpec((tm, d), lambda i: (i, 0)),
        out_shape=jax.ShapeDtypeStruct((rows, d), F32),
        scratch_shapes=[pltpu.VMEM((tm, d), BF16), pltpu.VMEM((tm, d), F32)],
        compiler_params=_compiler_params(1),
        name=f"ffn{j}",
    )(x, mod, g_pre, g_post, w1g, w1u, w2)


_IN_BLOCKS = ("aq", "ak", "av", "bq", "bk", "bv", "cu", "cv", "dq", "dkr", "dkv")
_IN_WIDTHS = dict(aq=256, ak=256, av=256, bq=256, bk=256, bv=256, cu=256, cv=256,
                  dq=256, dkr=128, dkv=128)
_IN_OFFSETS = {}
_off = 0
for _n in _IN_BLOCKS:
    _IN_OFFSETS[_n] = _off
    _off += _IN_WIDTHS[_n]
IN_EXT_WIDTH = _off


def _rope_slabs(x, tabs, shift, store):
    cos, s_up, s_dn = tabs
    for s in range(x.shape[1] // LANES):
        xs = x[:, s * LANES:(s + 1) * LANES]
        if cos is not None:
            xs = (xs * cos + pltpu.roll(xs, LANES - shift, axis=1) * s_up
                  + pltpu.roll(xs, shift, axis=1) * s_dn)
        store(s, xs.astype(BF16))


def _mix_in_kernel(*refs, rope):
    (x_ref, mod_ref, gpre_ref, win_ref, gq_ref, gk_ref, bd64_ref, lng_ref, lnb_ref, wsp_ref,
     bsp_ref, gqa_ref, wuq_ref, gkva_ref, wukvk_ref, wukvv_ref) = refs[:16]
    refs = refs[16:]
    if rope:
        tabs = [r[...] for r in refs[:9]]
        t32, t64, td = tabs[0:3], tabs[3:6], tabs[6:9]
        refs = refs[9:]
    else:
        t32 = t64 = td = (None, None, None)
    qa_ref, ka_ref, va_ref, qb_ref, kb_ref, vb_ref, cc_ref, qd_ref, kd_ref, vd_ref = refs

    xn = _pre_norm(x_ref[...], mod_ref, gpre_ref, 1).astype(BF16)

    def proj(name):
        o = _IN_OFFSETS[name]
        return _dot(xn, win_ref[:, o:o + _IN_WIDTHS[name]])

    def slab_store(ref):
        def store(s, v):
            ref[:, s * LANES:(s + 1) * LANES] = v
        return store

    _rope_slabs(proj("aq"), t32, 8, slab_store(qa_ref))
    _rope_slabs(proj("ak"), t32, 8, slab_store(ka_ref))
    va_ref[...] = proj("av").astype(BF16)

    def qk_norm(p, g_ref):
        return p * lax.rsqrt(_group_mean(p * p, bd64_ref) + NORM_EPS) * g_ref[...]

    _rope_slabs(qk_norm(proj("bq"), gq_ref), t64, 16, slab_store(qb_ref))
    _rope_slabs(qk_norm(proj("bk"), gk_ref), t64, 16, slab_store(kb_ref))
    vb_ref[...] = proj("bv").astype(BF16)

    u = _gelu_tanh(proj("cu"))
    v = _gelu_tanh(proj("cv"))
    mu = jnp.mean(v, axis=-1, keepdims=True)
    vc = v - mu
    vln = (vc * lax.rsqrt(jnp.mean(vc * vc, axis=-1, keepdims=True) + NORM_EPS) * lng_ref[...]
           + lnb_ref[...]).astype(BF16)
    width = vln.shape[1]
    gw = width // C_GROUPS
    lane = lax.broadcasted_iota(jnp.int32, (CHUNK, width), 1)
    wsp = wsp_ref[...]
    bsp = bsp_ref[...]
    for ci in range(vln.shape[0] // CHUNK):
        rows = slice(ci * CHUNK, (ci + 1) * CHUNK)
        allg = _dot(wsp, vln[rows, :])
        mixed = allg[0:CHUNK]
        for g in range(1, C_GROUPS):
            mixed = jnp.where(lane >= g * gw, allg[g * CHUNK:(g + 1) * CHUNK], mixed)
        cc_ref[rows, :] = (u[rows, :] * (mixed + bsp)).astype(BF16)

    cq = (_rms(proj("dq")) * gqa_ref[...]).astype(BF16)
    _rope_slabs(_dot(cq, wuq_ref[...]), td, 8, slab_store(qd_ref))
    ckv = (_rms(proj("dkv")) * gkva_ref[...]).astype(BF16)
    kr = proj("dkr")
    _rope_slabs(_dot(ckv, wukvk_ref[...]) + jnp.concatenate([kr] * D_HEADS, axis=1), td, 8,
                slab_store(kd_ref))
    vd_ref[...] = _dot(ckv, wukvv_ref[...]).astype(BF16)


def _mix_in(x, mod, g_pre, params, tables, *, rows_per_mod, seq):
    rows, d = x.shape
    tm = min(ROW_TILE, rows_per_mod)
    per_mod = rows_per_mod // tm
    rope = tables is not None
    n_mod_blocks = rows // rows_per_mod
    if rope:
        grid = (seq // tm, n_mod_blocks)
        row_map = lambda t, b: (b * per_mod + t, 0)
        mod_map = lambda t, b: (b, 0, 0)
        tab_specs = [pl.BlockSpec((tm, LANES), lambda t, b: (t, 0)) for _ in tables]
    else:
        grid = (rows // tm, 1)
        row_map = lambda i, _: (i, 0)
        mod_map = lambda i, _: (i // per_mod, 0, 0)
        tab_specs = []
    out_widths = (256, 256, 256, 256, 256, 256, 256, 512, 512, 256)
    return pl.pallas_call(
        functools.partial(_mix_in_kernel, rope=rope),
        grid=grid,
        in_specs=[pl.BlockSpec((tm, d), row_map), pl.BlockSpec((1, N_MOD, d), mod_map),
                  _resident(g_pre.shape)] + [_resident(p.shape) for p in params] + tab_specs,
        out_specs=[pl.BlockSpec((tm, w), row_map) for w in out_widths],
        out_shape=[jax.ShapeDtypeStruct((rows, w), BF16) for w in out_widths],
        compiler_params=_compiler_params(2),
        name="mix_in_rope" if rope else "mix_in",
    )(x, mod, g_pre, *params, *(tables or ()))


_ATTN_CFG = dict(
    a=dict(groups=1, maps=8, map_w=32, pair=True, rows=512),
    b=dict(groups=1, maps=4, map_w=64, pair=False, rows=512),
    d=dict(groups=2, maps=2, map_w=128, pair=False, rows=512),
)


def _lane_fold(x, op):
    parts = [x[:, j * LANES:(j + 1) * LANES] for j in range(x.shape[1] // LANES)]
    while len(parts) > 1:
        parts = [op(parts[i], parts[i + 1]) for i in range(0, len(parts) - 1, 2)] + (
            [parts[-1]] if len(parts) % 2 else [])
    return parts[0]


def _first_chunk_keys(n_keys):
    half = n_keys // 2
    if n_keys >= 4 * MXU_DIM:
        half = -(-half // MXU_DIM) * MXU_DIM
    return half


def _attn_kernel(*refs, cfg, n_src, tq, lam_init):
    q_ref = refs[0]
    src = refs[1:1 + 2 * n_src]
    refs = refs[1 + 2 * n_src:]
    if cfg["pair"]:
        lam_ref, gsub_ref, bd64_ref = refs[:3]
        refs = refs[3:]
    o_ref, k_scr, v_scr, s_a, s_b, m_scr, l_scr, acc_scr = refs

    off = 0
    for i in range(n_src):
        n = src[2 * i].shape[0]
        k_scr[off:off + n, :] = src[2 * i][...]
        v_scr[off:off + n, :] = src[2 * i + 1][...]
        off += n
    n_keys = off
    half = _first_chunk_keys(n_keys)
    chunk0, chunk1 = slice(0, half), slice(half, n_keys)

    maps, map_w, groups = cfg["maps"], cfg["map_w"], cfg["groups"]
    out_w = o_ref.shape[1]
    head_w = out_w // 4
    group_w = out_w // groups
    n_tiles = q_ref.shape[0] // tq
    lane_q = lax.broadcasted_iota(jnp.int32, (tq, MXU_DIM), 1)
    lane_o = lax.broadcasted_iota(jnp.int32, (tq, out_w), 1)

    if cfg["pair"]:
        lv = lam_ref[...]
        lam = (jnp.exp(jnp.sum(lv[0:1] * lv[1:2], axis=-1, keepdims=True))
               - jnp.exp(jnp.sum(lv[2:3] * lv[3:4], axis=-1, keepdims=True)) + lam_init)

    def row_max(s):
        return jnp.max(_lane_fold(s, jnp.maximum), axis=-1, keepdims=True)

    for g in range(groups):
        qcols = slice(g * MXU_DIM, (g + 1) * MXU_DIM)

        def scores(i, k_rows, s_dst):
            qg = q_ref[pl.ds(pl.multiple_of(i * tq, tq), tq), qcols]
            zero = jnp.zeros_like(qg)
            qs = jnp.concatenate(
                [jnp.where((lane_q >= m * map_w) & (lane_q < (m + 1) * map_w), qg, zero)
                 for m in range(maps)], axis=0)
            s_dst[...] = _dot_nt(qs, k_scr[k_rows, qcols])

        def first_chunk(s_src, k_rows):
            s = s_src[...]
            m_new = row_max(s)
            p = jnp.exp2(s - m_new)
            l_scr[...] = _lane_fold(p, jnp.add)
            acc_scr[...] = _dot(p.astype(BF16), v_scr[k_rows, :])
            m_scr[...] = jnp.broadcast_to(m_new, m_scr.shape)

        def second_chunk(s_src, k_rows):
            s = s_src[...]
            m_old = m_scr[...][:, :1]
            m_new = jnp.maximum(m_old, row_max(s))
            alpha = jnp.exp2(m_old - m_new)
            p = jnp.exp2(s - m_new)
            l = l_scr[...] * alpha + _lane_fold(p, jnp.add)
            acc = acc_scr[...] * alpha + _dot(p.astype(BF16), v_scr[k_rows, :])
            return acc / jnp.sum(l, axis=-1, keepdims=True)

        scores(0, chunk0, s_a)

        def q_tile(i, carry):
            scores(i, chunk1, s_b)
            first_chunk(s_a, chunk0)
            scores(jnp.minimum(i + 1, n_tiles - 1), chunk0, s_a)
            o = second_chunk(s_b, chunk1)

            out = jnp.zeros((tq, out_w), F32)
            for mi in range(maps):
                om = o[mi * tq:(mi + 1) * tq]
                if cfg["pair"]:
                    h, second = mi // 2, mi % 2
                    if second:
                        continue
                    om = om - lam * o[(mi + 1) * tq:(mi + 2) * tq]
                else:
                    h = g * maps + mi
                out = jnp.where((lane_o >= h * head_w) & (lane_o < (h + 1) * head_w), om, out)
            if cfg["pair"]:
                out = (out * lax.rsqrt(_group_mean(out * out, bd64_ref) + NORM_EPS) * gsub_ref[...]
                       * (1.0 - lam_init))
            gcols = slice(g * group_w, (g + 1) * group_w)
            o_ref[pl.ds(pl.multiple_of(i * tq, tq), tq), gcols] = out[:, gcols].astype(BF16)
            return carry

        lax.fori_loop(0, n_tiles, q_tile, 0)


def _attn(kind, q, sources, extra, *, n_batch, lam_init=0.0):
    cfg = _ATTN_CFG[kind]
    rows, qw = q.shape
    sq = rows // n_batch
    tq = min(cfg["rows"] // cfg["maps"], sq)
    stacked = cfg["maps"] * tq
    n_keys = sum(k.shape[0] // n_batch for k, _ in sources)
    out_w = sources[0][1].shape[1]
    in_specs = [pl.BlockSpec((sq, qw), lambda b: (b, 0))]
    args = [q]
    for k, v in sources:
        n = k.shape[0] // n_batch
        in_specs += [pl.BlockSpec((n, k.shape[1]), lambda b: (b, 0)),
                     pl.BlockSpec((n, v.shape[1]), lambda b: (b, 0))]
        args += [k, v]
    if cfg["pair"]:
        in_specs += [_resident(e.shape) for e in extra]
        args += list(extra)
    assert n_keys % (2 * LANES) == 0
    return pl.pallas_call(
        functools.partial(_attn_kernel, cfg=cfg, n_src=len(sources), tq=tq, lam_init=lam_init),
        grid=(n_batch,),
        in_specs=in_specs,
        out_specs=pl.BlockSpec((sq, out_w), lambda b: (b, 0)),
        out_shape=jax.ShapeDtypeStruct((rows, out_w), BF16),
        scratch_shapes=[pltpu.VMEM((n_keys, qw), BF16), pltpu.VMEM((n_keys, out_w), BF16),
                        pltpu.VMEM((stacked, _first_chunk_keys(n_keys)), F32),
                        pltpu.VMEM((stacked, n_keys - _first_chunk_keys(n_keys)), F32),
                        pltpu.VMEM((stacked, LANES), F32), pltpu.VMEM((stacked, LANES), F32),
                        pltpu.VMEM((stacked, out_w), F32)],
        compiler_params=_compiler_params(1),
        name=f"attn_{kind}_{len(sources)}",
    )(*args)


def _mix_out_kernel(x_ref, a_ref, b_ref, c_ref, d_ref, wout_ref, mod_ref, gpost_ref, o_ref):
    y = _dot(a_ref[...], wout_ref[0])
    y += _dot(b_ref[...], wout_ref[1])
    y += _dot(c_ref[...], wout_ref[2])
    y += _dot(d_ref[...], wout_ref[3])
    o_ref[...] = _post_norm(x_ref[...], y, mod_ref, gpost_ref, 1, 1.0)


def _mix_out(x, groups, w_out, mod, g_post, *, rows_per_mod):
    rows, d = x.shape
    tm = min(ROW_TILE, rows_per_mod)
    per_mod = rows_per_mod // tm
    gw = groups[0].shape[1]
    return pl.pallas_call(
        _mix_out_kernel,
        grid=(rows // tm,),
        in_specs=[pl.BlockSpec((tm, d), lambda i: (i, 0))]
                 + [pl.BlockSpec((tm, gw), lambda i: (i, 0)) for _ in groups]
                 + [_resident(w_out.shape),
                    pl.BlockSpec((1, N_MOD, d), lambda i: (i // per_mod, 0, 0)),
                    _resident(g_post.shape)],
        out_specs=pl.BlockSpec((tm, d), lambda i: (i, 0)),
        out_shape=jax.ShapeDtypeStruct((rows, d), F32),
        compiler_params=_compiler_params(1),
        name="mix_out",
    )(x, *groups, w_out, mod, g_post)


def _rope_table(seq, pattern):
    t = jnp.arange(seq, dtype=jnp.int32)
    pos = (jnp.floor_divide(t, GRID_W).astype(F32), jnp.remainder(t, GRID_W).astype(F32))
    n_freq = next(e[2] for e in pattern if e is not None)
    axis_dim = 2 * n_freq
    inv_freq = ROPE_THETA ** (-jnp.arange(0, axis_dim, 2, dtype=F32) / axis_dim)
    active = np.array([e is not None for e in pattern])
    axis = np.array([e[0] if e is not None else 0 for e in pattern])
    freq = np.array([e[1] if e is not None else 0 for e in pattern])
    first = np.array([bool(e[3]) if e is not None else False for e in pattern])
    ang = jnp.where(axis[None, :] == 0, pos[0][:, None], pos[1][:, None]) * inv_freq[freq][None, :]
    cos = jnp.where(active[None, :], jnp.cos(ang), 1.0)
    sin = jnp.where(active[None, :], jnp.sin(ang), 0.0)
    return cos, jnp.where(first[None, :], -sin, 0.0), jnp.where(first[None, :], 0.0, sin)


def _rope_pattern(rot_dim):
    n_freq = rot_dim // 4
    return [(r // (2 * n_freq), r % n_freq, n_freq, (r % (2 * n_freq)) < n_freq)
            for r in range(rot_dim)]


def _rope_tables(seq):
    p32 = _rope_pattern(32) * (LANES // 32)
    p64 = _rope_pattern(64) * (LANES // 64)
    pd = [None] * 64 + _rope_pattern(32) + [None] * 32
    return _rope_table(seq, p32) + _rope_table(seq, p64) + _rope_table(seq, pd)


def _layer_params(l, w_in, g_qnorm, g_knorm, w_spatial, b_spatial, ln_g, ln_b, g_q_a, w_uq,
                  g_kv_a, w_ukv):
    d_model = w_in.shape[1]
    sizes = (256, 256, 256, 256, 128, 128, 256, 256, 256, 128, 32)
    offs = [0]
    for s in sizes:
        offs.append(offs[-1] + s)
    aq, ak, av, bq, bk, bv, cu, cv, dq, dkv, dkr = (w_in[l][:, offs[i]:offs[i + 1]]
                                                    for i in range(11))
    log2e = math.log2(math.e)
    a_scale = 32 ** -0.5 * log2e
    b_scale = 64 ** -0.5 * log2e
    d_scale = 96 ** -0.5 * log2e

    def rep_kv(w):
        hd = w.shape[1] // B_KV_HEADS
        n_rep = B_HEADS // B_KV_HEADS
        return jnp.concatenate([w[:, (h // n_rep) * hd:(h // n_rep + 1) * hd]
                                for h in range(B_HEADS)], axis=1)

    zeros = lambda n: jnp.zeros((d_model, n), w_in.dtype)
    dkr_placed = jnp.concatenate([zeros(64), dkr, zeros(32)], axis=1)
    blocks = dict(aq=aq * a_scale, ak=ak, av=av, bq=bq, bk=rep_kv(bk), bv=rep_kv(bv), cu=cu, cv=cv,
                  dq=dq, dkr=dkr_placed, dkv=dkv)
    w_ext = jnp.concatenate([blocks[n] for n in _IN_BLOCKS], axis=1).astype(BF16)

    def tile_row(v, n, scale=1.0):
        return (jnp.tile(v, n) * scale).reshape(1, -1)

    bd64 = jnp.kron(jnp.eye(4, dtype=F32), jnp.full((64, 64), 1.0 / 64, F32)).astype(BF16)
    wsp = w_spatial[l].reshape(C_GROUPS * CHUNK, CHUNK).astype(BF16)
    bsp = jnp.repeat(b_spatial[l].T, 256 // C_GROUPS, axis=1)

    uq = w_uq[l].reshape(-1, D_HEADS, 96) * d_scale
    uq = jnp.concatenate([uq, jnp.zeros(uq.shape[:2] + (32,), uq.dtype)], axis=-1)
    wuq = uq.reshape(uq.shape[0], D_HEADS * LANES).astype(BF16)
    ukv = w_ukv[l].reshape(-1, D_HEADS, 128)
    ukv_k = jnp.concatenate([ukv[..., :64], jnp.zeros(ukv.shape[:2] + (64,), ukv.dtype)], axis=-1)
    wukvk = ukv_k.reshape(ukv.shape[0], D_HEADS * LANES).astype(BF16)
    wukvv = ukv[..., 64:].reshape(ukv.shape[0], D_HEADS * 64).astype(BF16)

    return (w_ext, tile_row(g_qnorm[l], B_HEADS, b_scale), tile_row(g_knorm[l], B_HEADS), bd64,
            ln_g[l].reshape(1, -1), ln_b[l].reshape(1, -1), wsp, bsp,
            g_q_a[l].reshape(1, -1), wuq, g_kv_a[l].reshape(1, -1), wukvk, wukvv), bd64


def _ffn_weights(w_in, w_out):
    d, two_ff = w_in.shape
    d_ff = two_ff // 2
    n = d_ff // FF_TILE
    w1g = w_in[:, :d_ff].reshape(d, n, FF_TILE).transpose(1, 0, 2).astype(BF16)
    w1u = w_in[:, d_ff:].reshape(d, n, FF_TILE).transpose(1, 0, 2).astype(BF16)
    w2 = w_out.reshape(n, FF_TILE, d).astype(BF16)
    return w1g, w1u, w2


def _lambda_init(layer_idx):
    return 0.8 - 0.6 * math.exp(-0.3 * layer_idx)


def kernel(x, c, ctx, c_ctx, w_ada, b_ada, g_pre, g_post, w_ffn1_in, w_ffn1_out, w_ffn2_in,
           w_ffn2_out, w_in, w_out, lam_vecs, g_subln, g_qnorm, g_knorm, w_spatial, b_spatial,
           ln_g, ln_b, g_q_a, w_uq, g_kv_a, w_ukv):
    n_batch, seq, d = x.shape
    n_ctx = ctx.shape[1]
    depth = w_ada.shape[0]

    n_cond = n_batch + 1
    pad = (-n_cond) % 8
    cc = jnp.concatenate([c, c_ctx[None, :], jnp.zeros((pad, d), c.dtype)], axis=0)
    mod = _ada(cc, w_ada, b_ada)
    mod_lat = mod[:, :n_batch].reshape(depth, n_batch, N_MOD, d)
    mod_ctx = mod[:, n_batch:n_cond].reshape(depth, 1, N_MOD, d)

    tables = _rope_tables(seq)
    x_lat = x.reshape(n_batch * seq, d)
    x_ctx = ctx.reshape(n_batch * n_ctx, d)
    lat = dict(rows_per_mod=seq)
    ctxk = dict(rows_per_mod=n_batch * n_ctx)

    for l in range(depth):
        need_ctx = l < depth - 1
        ffn1 = _ffn_weights(w_ffn1_in[l], w_ffn1_out[l])
        ffn2 = _ffn_weights(w_ffn2_in[l], w_ffn2_out[l])
        params, bd64 = _layer_params(l, w_in, g_qnorm, g_knorm, w_spatial, b_spatial, ln_g, ln_b,
                                     g_q_a, w_uq, g_kv_a, w_ukv)
        wo = w_out[l].reshape(N_MIXERS, -1, d).astype(BF16)
        a_extra = (lam_vecs[l], jnp.tile(g_subln[l], A_HEADS).reshape(1, -1), bd64)
        lam0 = _lambda_init(l)

        x_ctx = _ffn(x_ctx, mod_ctx[l], g_pre[l], g_post[l], *ffn1, j=0, **ctxk)
        x_lat = _ffn(x_lat, mod_lat[l], g_pre[l], g_post[l], *ffn1, j=0, **lat)

        pc = _mix_in(x_ctx, mod_ctx[l], g_pre[l], params, None, seq=n_ctx, **ctxk)
        pl_ = _mix_in(x_lat, mod_lat[l], g_pre[l], params, tables, seq=seq, **lat)
        qa_c, ka_c, va_c, qb_c, kb_c, vb_c, cc_c, qd_c, kd_c, vd_c = pc
        qa_l, ka_l, va_l, qb_l, kb_l, vb_l, cc_l, qd_l, kd_l, vd_l = pl_

        a_l = _attn("a", qa_l, [(ka_l, va_l), (ka_c, va_c)], a_extra, n_batch=n_batch, lam_init=lam0)
        b_l = _attn("b", qb_l, [(kb_l, vb_l), (kb_c, vb_c)], (), n_batch=n_batch)
        d_l = _attn("d", qd_l, [(kd_l, vd_l), (kd_c, vd_c)], (), n_batch=n_batch)
        x_lat = _mix_out(x_lat, (a_l, b_l, cc_l, d_l), wo, mod_lat[l], g_post[l], **lat)
        x_lat = _ffn(x_lat, mod_lat[l], g_pre[l], g_post[l], *ffn2, j=2, **lat)

        if need_ctx:
            a_c = _attn("a", qa_c, [(ka_c, va_c)], a_extra, n_batch=n_batch, lam_init=lam0)
            b_c = _attn("b", qb_c, [(kb_c, vb_c)], (), n_batch=n_batch)
            d_c = _attn("d", qd_c, [(kd_c, vd_c)], (), n_batch=n_batch)
            x_ctx = _mix_out(x_ctx, (a_c, b_c, cc_c, d_c), wo, mod_ctx[l], g_post[l], **ctxk)
            x_ctx = _ffn(x_ctx, mod_ctx[l], g_pre[l], g_post[l], *ffn2, j=2, **ctxk)

    return x_lat.reshape(n_batch, seq, d)
```

```python
import functools
import math

import jax
import jax.numpy as jnp
import numpy as np
from jax import lax
from jax.experimental import pallas as pl
from jax.experimental.pallas import tpu as pltpu

F32 = jnp.float32
BF16 = jnp.bfloat16

GRID_W = 64
CHUNK = 128
ROPE_THETA = 10000.0
NORM_EPS = 1e-6
FFN_RES = 0.5
N_MOD = 9
N_MIXERS = 4
A_HEADS, B_HEADS, B_KV_HEADS, C_GROUPS, D_HEADS = 4, 4, 2, 4, 4

LANES = 128
MXU_DIM = 256
VMEM_LIMIT_BYTES = 58 * 2**20

ROW_TILE = 1024
FF_TILE = 256


def _dot(a, b):
    return jnp.dot(a, b, preferred_element_type=F32)


def _dot_nt(a, b):
    return lax.dot_general(a, b, (((1,), (1,)), ((), ())), preferred_element_type=F32)


def _split_bf16(x):
    hi = x.astype(BF16)
    lo = (x - hi.astype(F32)).astype(BF16)
    return hi, lo


def _group_mean(xsq, bd_ref):
    hi, lo = _split_bf16(xsq)
    bd = bd_ref[...]
    return _dot(hi, bd) + _dot(lo, bd)


def _rms(x):
    return x * lax.rsqrt(jnp.mean(x * x, axis=-1, keepdims=True) + NORM_EPS)


def _pre_norm(x, mod_ref, gpre_ref, j):
    shift = mod_ref[0, 3 * j:3 * j + 1, :]
    gain = gpre_ref[j:j + 1, :] * (1.0 + mod_ref[0, 3 * j + 1:3 * j + 2, :])
    return _rms(x) * gain + shift


def _post_norm(x, y, mod_ref, gpost_ref, j, res):
    gain = (res * mod_ref[0, 3 * j + 2:3 * j + 3, :]) * gpost_ref[j:j + 1, :]
    return x + _rms(y) * gain


def _gelu_tanh(x):
    c = math.sqrt(2.0 / math.pi)
    return x * (0.5 * (1.0 + jnp.tanh(c * (x + 0.044715 * (x * x * x)))))


def _compiler_params(n_axes):
    return pltpu.CompilerParams(dimension_semantics=("arbitrary",) * n_axes,
                                vmem_limit_bytes=VMEM_LIMIT_BYTES)


def _resident(shape):
    nd = len(shape)
    return pl.BlockSpec(shape, lambda *_: (0,) * nd, pipeline_mode=pl.Buffered(1))


def _ada_kernel(c_ref, w_ref, b_ref, o_ref):
    c = c_ref[...]
    s = c * jax.nn.sigmoid(c)
    s_hi, s_lo = _split_bf16(s)
    w_hi, w_lo = _split_bf16(w_ref[0])
    o_ref[0] = _dot(s_hi, w_hi) + (_dot(s_hi, w_lo) + _dot(s_lo, w_hi)) + b_ref[0]


def _ada(cc, w_ada, b_ada):
    n_layers, d, n = w_ada.shape
    rows = cc.shape[0]
    tn = 1024
    return pl.pallas_call(
        _ada_kernel,
        grid=(n_layers, n // tn),
        in_specs=[pl.BlockSpec((rows, d), lambda l, j: (0, 0)),
                  pl.BlockSpec((1, d, tn), lambda l, j: (l, 0, j)),
                  pl.BlockSpec((1, 1, tn), lambda l, j: (l, 0, j))],
        out_specs=pl.BlockSpec((1, rows, tn), lambda l, j: (l, 0, j)),
        out_shape=jax.ShapeDtypeStruct((n_layers, rows, n), F32),
        compiler_params=_compiler_params(2),
        name="ada",
    )(cc, w_ada, b_ada.reshape(n_layers, 1, n))


def _ffn_kernel(x_ref, mod_ref, gpre_ref, gpost_ref, w1g_ref, w1u_ref, w2_ref, o_ref,
                xn_scr, acc_scr, *, j):
    x = x_ref[...]
    xn_scr[...] = _pre_norm(x, mod_ref, gpre_ref, j).astype(BF16)
    acc_scr[...] = jnp.zeros(acc_scr.shape, F32)

    def body(c, carry):
        xn = xn_scr[...]
        hg = _dot(xn, w1g_ref[c])
        hu = _dot(xn, w1u_ref[c])
        act = ((hg * jax.nn.sigmoid(hg)) * hu).astype(BF16)
        acc_scr[...] += _dot(act, w2_ref[c])
        return carry

    lax.fori_loop(0, w1g_ref.shape[0], body, 0)
    o_ref[...] = _post_norm(x, acc_scr[...], mod_ref, gpost_ref, j, FFN_RES)


def _ffn(x, mod, g_pre, g_post, w1g, w1u, w2, *, j, rows_per_mod):
    rows, d = x.shape
    tm = min(ROW_TILE, rows_per_mod)
    per_mod = rows_per_mod // tm
    return pl.pallas_call(
        functools.partial(_ffn_kernel, j=j),
        grid=(rows // tm,),
        in_specs=[pl.BlockSpec((tm, d), lambda i: (i, 0)),
                  pl.BlockSpec((1, N_MOD, d), lambda i: (i // per_mod, 0, 0)),
                  _resident(g_pre.shape), _resident(g_post.shape),
                  _resident(w1g.shape), _resident(w1u.shape), _resident(w2.shape)],
        out_specs=pl.BlockSpec((tm, d), lambda i: (i, 0)),
        out_shape=jax.ShapeDtypeStruct((rows, d), F32),
        scratch_shapes=[pltpu.VMEM((tm, d), BF16), pltpu.VMEM((tm, d), F32)],
        compiler_params=_compiler_params(1),
        name=f"ffn{j}",
    )(x, mod, g_pre, g_post, w1g, w1u, w2)


_IN_BLOCKS = ("aq", "ak", "av", "bq", "bk", "bv", "cu", "cv", "dq", "dkr", "dkv")
_IN_WIDTHS = dict(aq=256, ak=256, av=256, bq=256, bk=256, bv=256, cu=256, cv=256,
                  dq=256, dkr=128, dkv=128)
_IN_OFFSETS = {}
_off = 0
for _n in _IN_BLOCKS:
    _IN_OFFSETS[_n] = _off
    _off += _IN_WIDTHS[_n]
IN_EXT_WIDTH = _off


def _rope_slabs(x, tabs, shift, store):
    cos, s_up, s_dn = tabs
    for s in range(x.shape[1] // LANES):
        xs = x[:, s * LANES:(s + 1) * LANES]
        if cos is not None:
            xs = (xs * cos + pltpu.roll(xs, LANES - shift, axis=1) * s_up
                  + pltpu.roll(xs, shift, axis=1) * s_dn)
        store(s, xs.astype(BF16))


def _mix_in_kernel(*refs, rope):
    (x_ref, mod_ref, gpre_ref, win_ref, gq_ref, gk_ref, bd64_ref, lng_ref, lnb_ref, wsp_ref,
     bsp_ref, gqa_ref, wuq_ref, gkva_ref, wukvk_ref, wukvv_ref) = refs[:16]
    refs = refs[16:]
    if rope:
        tabs = [r[...] for r in refs[:9]]
        t32, t64, td = tabs[0:3], tabs[3:6], tabs[6:9]
        refs = refs[9:]
    else:
        t32 = t64 = td = (None, None, None)
    qa_ref, ka_ref, va_ref, qb_ref, kb_ref, vb_ref, cc_ref, qd_ref, kd_ref, vd_ref = refs

    xn = _pre_norm(x_ref[...], mod_ref, gpre_ref, 1).astype(BF16)

    def proj(name):
        o = _IN_OFFSETS[name]
        return _dot(xn, win_ref[:, o:o + _IN_WIDTHS[name]])

    def slab_store(ref):
        def store(s, v):
            ref[:, s * LANES:(s + 1) * LANES] = v
        return store

    _rope_slabs(proj("aq"), t32, 8, slab_store(qa_ref))
    _rope_slabs(proj("ak"), t32, 8, slab_store(ka_ref))
    va_ref[...] = proj("av").astype(BF16)

    def qk_norm(p, g_ref):
        return p * lax.rsqrt(_group_mean(p * p, bd64_ref) + NORM_EPS) * g_ref[...]

    _rope_slabs(qk_norm(proj("bq"), gq_ref), t64, 16, slab_store(qb_ref))
    _rope_slabs(qk_norm(proj("bk"), gk_ref), t64, 16, slab_store(kb_ref))
    vb_ref[...] = proj("bv").astype(BF16)

    u = _gelu_tanh(proj("cu"))
    v = _gelu_tanh(proj("cv"))
    mu = jnp.mean(v, axis=-1, keepdims=True)
    vc = v - mu
    vln = (vc * lax.rsqrt(jnp.mean(vc * vc, axis=-1, keepdims=True) + NORM_EPS) * lng_ref[...]
           + lnb_ref[...]).astype(BF16)
    width = vln.shape[1]
    gw = width // C_GROUPS
    lane = lax.broadcasted_iota(jnp.int32, (CHUNK, width), 1)
    wsp = wsp_ref[...]
    bsp = bsp_ref[...]
    for ci in range(vln.shape[0] // CHUNK):
        rows = slice(ci * CHUNK, (ci + 1) * CHUNK)
        allg = _dot(wsp, vln[rows, :])
        mixed = allg[0:CHUNK]
        for g in range(1, C_GROUPS):
            mixed = jnp.where(lane >= g * gw, allg[g * CHUNK:(g + 1) * CHUNK], mixed)
        cc_ref[rows, :] = (u[rows, :] * (mixed + bsp)).astype(BF16)

    cq = (_rms(proj("dq")) * gqa_ref[...]).astype(BF16)
    _rope_slabs(_dot(cq, wuq_ref[...]), td, 8, slab_store(qd_ref))
    ckv = (_rms(proj("dkv")) * gkva_ref[...]).astype(BF16)
    kr = proj("dkr")
    _rope_slabs(_dot(ckv, wukvk_ref[...]) + jnp.concatenate([kr] * D_HEADS, axis=1), td, 8,
                slab_store(kd_ref))
    vd_ref[...] = _dot(ckv, wukvv_ref[...]).astype(BF16)


def _mix_in(x, mod, g_pre, params, tables, *, rows_per_mod, seq):
    rows, d = x.shape
    tm = min(ROW_TILE, rows_per_mod)
    per_mod = rows_per_mod // tm
    rope = tables is not None
    n_mod_blocks = rows // rows_per_mod
    if rope:
        grid = (seq // tm, n_mod_blocks)
        row_map = lambda t, b: (b * per_mod + t, 0)
        mod_map = lambda t, b: (b, 0, 0)
        tab_specs = [pl.BlockSpec((tm, LANES), lambda t, b: (t, 0)) for _ in tables]
    else:
        grid = (rows // tm, 1)
        row_map = lambda i, _: (i, 0)
        mod_map = lambda i, _: (i // per_mod, 0, 0)
        tab_specs = []
    out_widths = (256, 256, 256, 256, 256, 256, 256, 512, 512, 256)
    return pl.pallas_call(
        functools.partial(_mix_in_kernel, rope=rope),
        grid=grid,
        in_specs=[pl.BlockSpec((tm, d), row_map), pl.BlockSpec((1, N_MOD, d), mod_map),
                  _resident(g_pre.shape)] + [_resident(p.shape) for p in params] + tab_specs,
        out_specs=[pl.BlockSpec((tm, w), row_map) for w in out_widths],
        out_shape=[jax.ShapeDtypeStruct((rows, w), BF16) for w in out_widths],
        compiler_params=_compiler_params(2),
        name="mix_in_rope" if rope else "mix_in",
    )(x, mod, g_pre, *params, *(tables or ()))


def _attn_groups(kind):
    if kind == "a":
        return [(0, [(h * 64 + m * 32, h * 64 + (m + 1) * 32, h) for h in (2 * g, 2 * g + 1)
                     for m in (0, 1)]) for g in range(2)]
    if kind == "b":
        return [(0, [(h * 64, (h + 1) * 64, h) for h in (2 * g, 2 * g + 1)]) for g in range(2)]
    if kind == "d":
        return [(g, [(i * LANES, (i + 1) * LANES, 2 * g + i) for i in range(2)]) for g in range(2)]
    raise ValueError(kind)


ATTN_Q_TILE = 256
HEAD_W = 64


def _first_chunk_keys(n_keys):
    half = n_keys // 2
    if n_keys >= 4 * MXU_DIM:
        half = -(-half // MXU_DIM) * MXU_DIM
    return half


def _attn_kernel(*refs, groups, n_src, tq, pair, lam_init):
    q_ref = refs[0]
    src = refs[1:1 + 2 * n_src]
    refs = refs[1 + 2 * n_src:]
    if pair:
        lam_ref, gsub_ref = refs[:2]
        refs = refs[2:]
    o_ref, k_scr, vt_scr, s_a, s_b = refs

    off = 0
    for i in range(n_src):
        n = src[2 * i].shape[0]
        k_scr[off:off + n, :] = src[2 * i][...]
        for r in range(0, n, MXU_DIM):
            blk = src[2 * i + 1][r:r + MXU_DIM, :].astype(F32)
            vt_scr[:, off + r:off + r + blk.shape[0]] = blk.T.astype(BF16)
        off += n
    n_keys = off
    half = _first_chunk_keys(n_keys)
    chunk0, chunk1 = slice(0, half), slice(half, n_keys)
    n_tiles = q_ref.shape[0] // tq
    lane_q = lax.broadcasted_iota(jnp.int32, (tq, MXU_DIM), 1)

    if pair:
        lv = lam_ref[...]
        lam = (jnp.exp(jnp.sum(lv[0:1] * lv[1:2], axis=-1, keepdims=True))
               - jnp.exp(jnp.sum(lv[2:3] * lv[3:4], axis=-1, keepdims=True)) + lam_init)

    for slab, maps in groups:
        qcols = slice(slab * MXU_DIM, (slab + 1) * MXU_DIM)
        heads = sorted({h for _, _, h in maps})

        def scores_t(i, k_rows, s_dst):
            qg = q_ref[pl.ds(pl.multiple_of(i * tq, tq), tq), qcols]
            zero = jnp.zeros_like(qg)
            qs = jnp.concatenate([jnp.where((lane_q >= lo) & (lane_q < hi), qg, zero)
                                  for lo, hi, _ in maps], axis=0)
            s_dst[...] = _dot_nt(k_scr[k_rows, qcols], qs)

        def weigh(s, m, k_cols):
            p = jnp.exp2(s - m)
            l = jnp.sum(p, axis=0, keepdims=True)
            pb = p.astype(BF16)
            acc = [_dot(vt_scr[h * HEAD_W:(h + 1) * HEAD_W, k_cols], pb[:, j * tq:(j + 1) * tq])
                   for j, (_, _, h) in enumerate(maps)]
            return l, acc

        scores_t(0, chunk0, s_a)

        def q_tile(i, carry):
            scores_t(i, chunk1, s_b)
            s0 = s_a[...]
            m0 = jnp.max(s0, axis=0, keepdims=True)
            l0, acc0 = weigh(s0, m0, chunk0)
            scores_t(jnp.minimum(i + 1, n_tiles - 1), chunk0, s_a)
            s1 = s_b[...]
            m1 = jnp.maximum(m0, jnp.max(s1, axis=0, keepdims=True))
            alpha = jnp.exp2(m0 - m1)
            l1, acc1 = weigh(s1, m1, chunk1)
            inv_l = 1.0 / (l0 * alpha + l1)

            def out_t(j):
                cols = slice(j * tq, (j + 1) * tq)
                return (acc0[j] * alpha[:, cols] + acc1[j]) * inv_l[:, cols]

            rows = []
            for h in heads:
                js = [j for j, (_, _, hh) in enumerate(maps) if hh == h]
                if pair:
                    o = out_t(js[0]) - lam * out_t(js[1])
                    o = (o * lax.rsqrt(jnp.mean(o * o, axis=0, keepdims=True) + NORM_EPS)
                         * gsub_ref[...] * (1.0 - lam_init))
                else:
                    o = out_t(js[0])
                rows.append(o)
            out = jnp.concatenate(rows, axis=0).T
            ocols = slice(heads[0] * HEAD_W, (heads[-1] + 1) * HEAD_W)
            o_ref[pl.ds(pl.multiple_of(i * tq, tq), tq), ocols] = out.astype(BF16)
            return carry

        lax.fori_loop(0, n_tiles, q_tile, 0)


def _attn(kind, q, sources, extra, *, n_batch, lam_init=0.0):
    groups = _attn_groups(kind)
    pair = kind == "a"
    rows, qw = q.shape
    sq = rows // n_batch
    tq = min(ATTN_Q_TILE, sq)
    n_keys = sum(k.shape[0] // n_batch for k, _ in sources)
    out_w = sources[0][1].shape[1]
    cols = max(len(maps) for _, maps in groups) * tq
    in_specs = [pl.BlockSpec((sq, qw), lambda b: (b, 0))]
    args = [q]
    for k, v in sources:
        n = k.shape[0] // n_batch
        in_specs += [pl.BlockSpec((n, k.shape[1]), lambda b: (b, 0)),
                     pl.BlockSpec((n, v.shape[1]), lambda b: (b, 0))]
        args += [k, v]
    if pair:
        in_specs += [_resident(e.shape) for e in extra]
        args += list(extra)
    assert n_keys % (2 * LANES) == 0 and sq % tq == 0
    half = _first_chunk_keys(n_keys)
    return pl.pallas_call(
        functools.partial(_attn_kernel, groups=groups, n_src=len(sources), tq=tq, pair=pair,
                          lam_init=lam_init),
        grid=(n_batch,),
        in_specs=in_specs,
        out_specs=pl.BlockSpec((sq, out_w), lambda b: (b, 0)),
        out_shape=jax.ShapeDtypeStruct((rows, out_w), BF16),
        scratch_shapes=[pltpu.VMEM((n_keys, qw), BF16), pltpu.VMEM((out_w, n_keys), BF16),
                        pltpu.VMEM((half, cols), F32), pltpu.VMEM((n_keys - half, cols), F32)],
        compiler_params=_compiler_params(1),
        name=f"attn_{kind}_{len(sources)}",
    )(*args)


def _mix_out_kernel(x_ref, a_ref, b_ref, c_ref, d_ref, wout_ref, mod_ref, gpost_ref, o_ref):
    y = _dot(a_ref[...], wout_ref[0])
    y += _dot(b_ref[...], wout_ref[1])
    y += _dot(c_ref[...], wout_ref[2])
    y += _dot(d_ref[...], wout_ref[3])
    o_ref[...] = _post_norm(x_ref[...], y, mod_ref, gpost_ref, 1, 1.0)


def _mix_out(x, groups, w_out, mod, g_post, *, rows_per_mod):
    rows, d = x.shape
    tm = min(ROW_TILE, rows_per_mod)
    per_mod = rows_per_mod // tm
    gw = groups[0].shape[1]
    return pl.pallas_call(
        _mix_out_kernel,
        grid=(rows // tm,),
        in_specs=[pl.BlockSpec((tm, d), lambda i: (i, 0))]
                 + [pl.BlockSpec((tm, gw), lambda i: (i, 0)) for _ in groups]
                 + [_resident(w_out.shape),
                    pl.BlockSpec((1, N_MOD, d), lambda i: (i // per_mod, 0, 0)),
                    _resident(g_post.shape)],
        out_specs=pl.BlockSpec((tm, d), lambda i: (i, 0)),
        out_shape=jax.ShapeDtypeStruct((rows, d), F32),
        compiler_params=_compiler_params(1),
        name="mix_out",
    )(x, *groups, w_out, mod, g_post)


def _rope_table(seq, pattern):
    t = jnp.arange(seq, dtype=jnp.int32)
    pos = (jnp.floor_divide(t, GRID_W).astype(F32), jnp.remainder(t, GRID_W).astype(F32))
    n_freq = next(e[2] for e in pattern if e is not None)
    axis_dim = 2 * n_freq
    inv_freq = ROPE_THETA ** (-jnp.arange(0, axis_dim, 2, dtype=F32) / axis_dim)
    active = np.array([e is not None for e in pattern])
    axis = np.array([e[0] if e is not None else 0 for e in pattern])
    freq = np.array([e[1] if e is not None else 0 for e in pattern])
    first = np.array([bool(e[3]) if e is not None else False for e in pattern])
    ang = jnp.where(axis[None, :] == 0, pos[0][:, None], pos[1][:, None]) * inv_freq[freq][None, :]
    cos = jnp.where(active[None, :], jnp.cos(ang), 1.0)
    sin = jnp.where(active[None, :], jnp.sin(ang), 0.0)
    return cos, jnp.where(first[None, :], -sin, 0.0), jnp.where(first[None, :], 0.0, sin)


def _rope_pattern(rot_dim):
    n_freq = rot_dim // 4
    return [(r // (2 * n_freq), r % n_freq, n_freq, (r % (2 * n_freq)) < n_freq)
            for r in range(rot_dim)]


def _rope_tables(seq):
    p32 = _rope_pattern(32) * (LANES // 32)
    p64 = _rope_pattern(64) * (LANES // 64)
    pd = [None] * 64 + _rope_pattern(32) + [None] * 32
    return _rope_table(seq, p32) + _rope_table(seq, p64) + _rope_table(seq, pd)


def _layer_params(l, w_in, g_qnorm, g_knorm, w_spatial, b_spatial, ln_g, ln_b, g_q_a, w_uq,
                  g_kv_a, w_ukv):
    d_model = w_in.shape[1]
    sizes = (256, 256, 256, 256, 128, 128, 256, 256, 256, 128, 32)
    offs = [0]
    for s in sizes:
        offs.append(offs[-1] + s)
    aq, ak, av, bq, bk, bv, cu, cv, dq, dkv, dkr = (w_in[l][:, offs[i]:offs[i + 1]]
                                                    for i in range(11))
    log2e = math.log2(math.e)
    a_scale = 32 ** -0.5 * log2e
    b_scale = 64 ** -0.5 * log2e
    d_scale = 96 ** -0.5 * log2e

    def rep_kv(w):
        hd = w.shape[1] // B_KV_HEADS
        n_rep = B_HEADS // B_KV_HEADS
        return jnp.concatenate([w[:, (h // n_rep) * hd:(h // n_rep + 1) * hd]
                                for h in range(B_HEADS)], axis=1)

    zeros = lambda n: jnp.zeros((d_model, n), w_in.dtype)
    dkr_placed = jnp.concatenate([zeros(64), dkr, zeros(32)], axis=1)
    blocks = dict(aq=aq * a_scale, ak=ak, av=av, bq=bq, bk=rep_kv(bk), bv=rep_kv(bv), cu=cu, cv=cv,
                  dq=dq, dkr=dkr_placed, dkv=dkv)
    w_ext = jnp.concatenate([blocks[n] for n in _IN_BLOCKS], axis=1).astype(BF16)

    def tile_row(v, n, scale=1.0):
        return (jnp.tile(v, n) * scale).reshape(1, -1)

    bd64 = jnp.kron(jnp.eye(4, dtype=F32), jnp.full((64, 64), 1.0 / 64, F32)).astype(BF16)
    wsp = w_spatial[l].reshape(C_GROUPS * CHUNK, CHUNK).astype(BF16)
    bsp = jnp.repeat(b_spatial[l].T, 256 // C_GROUPS, axis=1)

    uq = w_uq[l].reshape(-1, D_HEADS, 96) * d_scale
    uq = jnp.concatenate([uq, jnp.zeros(uq.shape[:2] + (32,), uq.dtype)], axis=-1)
    wuq = uq.reshape(uq.shape[0], D_HEADS * LANES).astype(BF16)
    ukv = w_ukv[l].reshape(-1, D_HEADS, 128)
    ukv_k = jnp.concatenate([ukv[..., :64], jnp.zeros(ukv.shape[:2] + (64,), ukv.dtype)], axis=-1)
    wukvk = ukv_k.reshape(ukv.shape[0], D_HEADS * LANES).astype(BF16)
    wukvv = ukv[..., 64:].reshape(ukv.shape[0], D_HEADS * 64).astype(BF16)

    return (w_ext, tile_row(g_qnorm[l], B_HEADS, b_scale), tile_row(g_knorm[l], B_HEADS), bd64,
            ln_g[l].reshape(1, -1), ln_b[l].reshape(1, -1), wsp, bsp,
            g_q_a[l].reshape(1, -1), wuq, g_kv_a[l].reshape(1, -1), wukvk, wukvv), bd64


def _ffn_weights(w_in, w_out):
    d, two_ff = w_in.shape
    d_ff = two_ff // 2
    n = d_ff // FF_TILE
    w1g = w_in[:, :d_ff].reshape(d, n, FF_TILE).transpose(1, 0, 2).astype(BF16)
    w1u = w_in[:, d_ff:].reshape(d, n, FF_TILE).transpose(1, 0, 2).astype(BF16)
    w2 = w_out.reshape(n, FF_TILE, d).astype(BF16)
    return w1g, w1u, w2


def _lambda_init(layer_idx):
    return 0.8 - 0.6 * math.exp(-0.3 * layer_idx)


def kernel(x, c, ctx, c_ctx, w_ada, b_ada, g_pre, g_post, w_ffn1_in, w_ffn1_out, w_ffn2_in,
           w_ffn2_out, w_in, w_out, lam_vecs, g_subln, g_qnorm, g_knorm, w_spatial, b_spatial,
           ln_g, ln_b, g_q_a, w_uq, g_kv_a, w_ukv):
    n_batch, seq, d = x.shape
    n_ctx = ctx.shape[1]
    depth = w_ada.shape[0]

    n_cond = n_batch + 1
    pad = (-n_cond) % 8
    cc = jnp.concatenate([c, c_ctx[None, :], jnp.zeros((pad, d), c.dtype)], axis=0)
    mod = _ada(cc, w_ada, b_ada)
    mod_lat = mod[:, :n_batch].reshape(depth, n_batch, N_MOD, d)
    mod_ctx = mod[:, n_batch:n_cond].reshape(depth, 1, N_MOD, d)

    tables = _rope_tables(seq)
    x_lat = x.reshape(n_batch * seq, d)
    x_ctx = ctx.reshape(n_batch * n_ctx, d)
    lat = dict(rows_per_mod=seq)
    ctxk = dict(rows_per_mod=n_batch * n_ctx)

    for l in range(depth):
        need_ctx = l < depth - 1
        ffn1 = _ffn_weights(w_ffn1_in[l], w_ffn1_out[l])
        ffn2 = _ffn_weights(w_ffn2_in[l], w_ffn2_out[l])
        params, bd64 = _layer_params(l, w_in, g_qnorm, g_knorm, w_spatial, b_spatial, ln_g, ln_b,
                                     g_q_a, w_uq, g_kv_a, w_ukv)
        wo = w_out[l].reshape(N_MIXERS, -1, d).astype(BF16)
        a_extra = (lam_vecs[l], g_subln[l].reshape(-1, 1))
        lam0 = _lambda_init(l)

        x_ctx = _ffn(x_ctx, mod_ctx[l], g_pre[l], g_post[l], *ffn1, j=0, **ctxk)
        x_lat = _ffn(x_lat, mod_lat[l], g_pre[l], g_post[l], *ffn1, j=0, **lat)

        pc = _mix_in(x_ctx, mod_ctx[l], g_pre[l], params, None, seq=n_ctx, **ctxk)
        pl_ = _mix_in(x_lat, mod_lat[l], g_pre[l], params, tables, seq=seq, **lat)
        qa_c, ka_c, va_c, qb_c, kb_c, vb_c, cc_c, qd_c, kd_c, vd_c = pc
        qa_l, ka_l, va_l, qb_l, kb_l, vb_l, cc_l, qd_l, kd_l, vd_l = pl_

        a_l = _attn("a", qa_l, [(ka_l, va_l), (ka_c, va_c)], a_extra, n_batch=n_batch, lam_init=lam0)
        b_l = _attn("b", qb_l, [(kb_l, vb_l), (kb_c, vb_c)], (), n_batch=n_batch)
        d_l = _attn("d", qd_l, [(kd_l, vd_l), (kd_c, vd_c)], (), n_batch=n_batch)
        x_lat = _mix_out(x_lat, (a_l, b_l, cc_l, d_l), wo, mod_lat[l], g_post[l], **lat)
        x_lat = _ffn(x_lat, mod_lat[l], g_pre[l], g_post[l], *ffn2, j=2, **lat)

        if need_ctx:
            a_c = _attn("a", qa_c, [(ka_c, va_c)], a_extra, n_batch=n_batch, lam_init=lam0)
            b_c = _attn("b", qb_c, [(kb_c, vb_c)], (), n_batch=n_batch)
            d_c = _attn("d", qd_c, [(kd_c, vd_c)], (), n_batch=n_batch)
            x_ctx = _mix_out(x_ctx, (a_c, b_c, cc_c, d_c), wo, mod_ctx[l], g_post[l], **ctxk)
            x_ctx = _ffn(x_ctx, mod_ctx[l], g_pre[l], g_post[l], *ffn2, j=2, **ctxk)

    return x_lat.reshape(n_batch, seq, d)
```

```python
import functools
import math

import jax
import jax.numpy as jnp
import numpy as np
from jax import lax
from jax.experimental import pallas as pl
from jax.experimental.pallas import tpu as pltpu

F32 = jnp.float32
BF16 = jnp.bfloat16

GRID_W = 64
CHUNK = 128
ROPE_THETA = 10000.0
NORM_EPS = 1e-6
FFN_RES = 0.5
N_MOD = 9
N_MIXERS = 4
A_HEADS, B_HEADS, B_KV_HEADS, C_GROUPS, D_HEADS = 4, 4, 2, 4, 4

LANES = 128
MXU_DIM = 256
VMEM_LIMIT_BYTES = 58 * 2**20

ROW_TILE = 1024
FF_TILE = 768
MIX_SUB_ROWS = 1024


def _dot(a, b):
    return jnp.dot(a, b, preferred_element_type=F32)


def _dot_nt(a, b):
    return lax.dot_general(a, b, (((1,), (1,)), ((), ())), preferred_element_type=F32)


def _split_bf16(x):
    hi = x.astype(BF16)
    lo = (x - hi.astype(F32)).astype(BF16)
    return hi, lo


def _group_mean(xsq, bd_ref):
    hi, lo = _split_bf16(xsq)
    bd = bd_ref[...]
    return _dot(hi, bd) + _dot(lo, bd)


def _rms(x):
    return x * lax.rsqrt(jnp.mean(x * x, axis=-1, keepdims=True) + NORM_EPS)


def _pre_norm(x, mod_ref, gpre_ref, j):
    shift = mod_ref[0, 3 * j:3 * j + 1, :]
    gain = gpre_ref[j:j + 1, :] * (1.0 + mod_ref[0, 3 * j + 1:3 * j + 2, :])
    return _rms(x) * gain + shift


def _post_norm(x, y, mod_ref, gpost_ref, j, res):
    gain = (res * mod_ref[0, 3 * j + 2:3 * j + 3, :]) * gpost_ref[j:j + 1, :]
    return x + _rms(y) * gain


def _gelu_tanh(x):
    c = math.sqrt(2.0 / math.pi)
    return x * (0.5 * (1.0 + jnp.tanh(c * (x + 0.044715 * (x * x * x)))))


def _compiler_params(n_axes):
    return pltpu.CompilerParams(dimension_semantics=("arbitrary",) * n_axes,
                                vmem_limit_bytes=VMEM_LIMIT_BYTES)


def _resident(shape):
    nd = len(shape)
    return pl.BlockSpec(shape, lambda *_: (0,) * nd, pipeline_mode=pl.Buffered(1))


def _ada_kernel(c_ref, w_ref, b_ref, o_ref):
    c = c_ref[...]
    s = c * jax.nn.sigmoid(c)
    s_hi, s_lo = _split_bf16(s)
    w_hi, w_lo = _split_bf16(w_ref[0])
    o_ref[0] = _dot(s_hi, w_hi) + (_dot(s_hi, w_lo) + _dot(s_lo, w_hi)) + b_ref[0]


def _ada(cc, w_ada, b_ada):
    n_layers, d, n = w_ada.shape
    rows = cc.shape[0]
    tn = 1024
    return pl.pallas_call(
        _ada_kernel,
        grid=(n_layers, n // tn),
        in_specs=[pl.BlockSpec((rows, d), lambda l, j: (0, 0)),
                  pl.BlockSpec((1, d, tn), lambda l, j: (l, 0, j)),
                  pl.BlockSpec((1, 1, tn), lambda l, j: (l, 0, j))],
        out_specs=pl.BlockSpec((1, rows, tn), lambda l, j: (l, 0, j)),
        out_shape=jax.ShapeDtypeStruct((n_layers, rows, n), F32),
        compiler_params=_compiler_params(2),
        name="ada",
    )(cc, w_ada, b_ada.reshape(n_layers, 1, n))


def _ffn_kernel(x_ref, mod_ref, gpre_ref, gpost_ref, hg_ref, hu_ref, hd_ref, w1g_ref, w1u_ref, w2_ref,
                o_ref, xn_scr, acc_scr, *, j):
    def swiglu_chunk(xn, wg, wu, wd):
        hg = _dot(xn, wg)
        hu = _dot(xn, wu)
        return _dot(((hg * jax.nn.sigmoid(hg)) * hu).astype(BF16), wd)

    x = x_ref[...]
    xn = _pre_norm(x, mod_ref, gpre_ref, j).astype(BF16)
    xn_scr[...] = xn
    acc_scr[...] = swiglu_chunk(xn, hg_ref[...], hu_ref[...], hd_ref[...])

    def body(c, carry):
        acc_scr[...] += swiglu_chunk(xn_scr[...], w1g_ref[c], w1u_ref[c], w2_ref[c])
        return carry

    lax.fori_loop(0, w1g_ref.shape[0], body, 0)
    o_ref[...] = _post_norm(x, acc_scr[...], mod_ref, gpost_ref, j, FFN_RES)


def _ffn(x, mod, g_pre, g_post, weights, *, j, rows_per_mod):
    rows, d = x.shape
    tm = min(ROW_TILE, rows_per_mod)
    per_mod = rows_per_mod // tm
    return pl.pallas_call(
        functools.partial(_ffn_kernel, j=j),
        grid=(rows // tm,),
        in_specs=[pl.BlockSpec((tm, d), lambda i: (i, 0)),
                  pl.BlockSpec((1, N_MOD, d), lambda i: (i // per_mod, 0, 0)),
                  _resident(g_pre.shape), _resident(g_post.shape)]
                 + [_resident(w.shape) for w in weights],
        out_specs=pl.BlockSpec((tm, d), lambda i: (i, 0)),
        out_shape=jax.ShapeDtypeStruct((rows, d), F32),
        scratch_shapes=[pltpu.VMEM((tm, d), BF16), pltpu.VMEM((tm, d), F32)],
        compiler_params=_compiler_params(1),
        name=f"ffn{j}",
    )(x, mod, g_pre, g_post, *weights)


_IN_BLOCKS = ("aq", "ak", "av", "bq", "bk", "bv", "cu", "cv", "dq", "dkr", "dkv")
_IN_WIDTHS = dict(aq=256, ak=256, av=256, bq=256, bk=256, bv=256, cu=256, cv=256,
                  dq=256, dkr=128, dkv=128)
_IN_OFFSETS = {}
_off = 0
for _n in _IN_BLOCKS:
    _IN_OFFSETS[_n] = _off
    _off += _IN_WIDTHS[_n]
IN_EXT_WIDTH = _off


def _rope_slabs(x, tabs, shift, store):
    cos, s_up, s_dn = tabs
    for s in range(x.shape[1] // LANES):
        xs = x[:, s * LANES:(s + 1) * LANES]
        if cos is not None:
            xs = (xs * cos + pltpu.roll(xs, LANES - shift, axis=1) * s_up
                  + pltpu.roll(xs, shift, axis=1) * s_dn)
        store(s, xs.astype(BF16))


def _mix_in_kernel(*refs, rope):
    (x_ref, mod_ref, gpre_ref, win_ref, gq_ref, gk_ref, bd64_ref, lng_ref, lnb_ref, wsp_ref,
     bsp_ref, gqa_ref, wuq_ref, gkva_ref, wukvk_ref, wukvv_ref) = refs[:16]
    refs = refs[16:]
    tab_refs = refs[:9] if rope else ()
    refs = refs[9:] if rope else refs
    qa_ref, ka_ref, va_ref, qb_ref, kb_ref, vb_ref, cc_ref, qd_ref, kd_ref, vd_ref = refs
    sub = min(MIX_SUB_ROWS, x_ref.shape[0])

    def sub_block(r, carry):
        rows = pl.ds(pl.multiple_of(r * sub, sub), sub)
        if rope:
            tabs = [t[rows, :] for t in tab_refs]
            t32, t64, td = tabs[0:3], tabs[3:6], tabs[6:9]
        else:
            t32 = t64 = td = (None, None, None)

        xn = _pre_norm(x_ref[rows, :], mod_ref, gpre_ref, 1).astype(BF16)

        def proj(name):
            o = _IN_OFFSETS[name]
            return _dot(xn, win_ref[:, o:o + _IN_WIDTHS[name]])

        def slab_store(ref):
            def store(s, v):
                ref[rows, s * LANES:(s + 1) * LANES] = v
            return store

        _rope_slabs(proj("aq"), t32, 8, slab_store(qa_ref))
        _rope_slabs(proj("ak"), t32, 8, slab_store(ka_ref))
        va_ref[rows, :] = proj("av").astype(BF16)

        def qk_norm(p, g_ref):
            return p * lax.rsqrt(_group_mean(p * p, bd64_ref) + NORM_EPS) * g_ref[...]

        _rope_slabs(qk_norm(proj("bq"), gq_ref), t64, 16, slab_store(qb_ref))
        _rope_slabs(qk_norm(proj("bk"), gk_ref), t64, 16, slab_store(kb_ref))
        vb_ref[rows, :] = proj("bv").astype(BF16)

        u = _gelu_tanh(proj("cu"))
        v = _gelu_tanh(proj("cv"))
        mu = jnp.mean(v, axis=-1, keepdims=True)
        vc = v - mu
        vln = (vc * lax.rsqrt(jnp.mean(vc * vc, axis=-1, keepdims=True) + NORM_EPS) * lng_ref[...]
               + lnb_ref[...]).astype(BF16)
        width = vln.shape[1]
        gw = width // C_GROUPS
        lane = lax.broadcasted_iota(jnp.int32, (CHUNK, width), 1)
        wsp = wsp_ref[...]
        bsp = bsp_ref[...]
        gated = []
        for ci in range(sub // CHUNK):
            crows = slice(ci * CHUNK, (ci + 1) * CHUNK)
            allg = _dot(wsp, vln[crows, :])
            mixed = allg[0:CHUNK]
            for g in range(1, C_GROUPS):
                mixed = jnp.where(lane >= g * gw, allg[g * CHUNK:(g + 1) * CHUNK], mixed)
            gated.append((u[crows, :] * (mixed + bsp)).astype(BF16))
        cc_ref[rows, :] = jnp.concatenate(gated, axis=0)

        cq = (_rms(proj("dq")) * gqa_ref[...]).astype(BF16)
        _rope_slabs(_dot(cq, wuq_ref[...]), td, 8, slab_store(qd_ref))
        ckv = (_rms(proj("dkv")) * gkva_ref[...]).astype(BF16)
        kr = proj("dkr")
        _rope_slabs(_dot(ckv, wukvk_ref[...]) + jnp.concatenate([kr] * D_HEADS, axis=1), td, 8,
                    slab_store(kd_ref))
        vd_ref[rows, :] = _dot(ckv, wukvv_ref[...]).astype(BF16)
        return carry

    lax.fori_loop(0, x_ref.shape[0] // sub, sub_block, 0)


def _mix_in(x, mod, g_pre, params, tables, *, rows_per_mod, seq):
    rows, d = x.shape
    tm = min(ROW_TILE, rows_per_mod)
    per_mod = rows_per_mod // tm
    rope = tables is not None
    n_mod_blocks = rows // rows_per_mod
    if rope:
        grid = (seq // tm, n_mod_blocks)
        row_map = lambda t, b: (b * per_mod + t, 0)
        mod_map = lambda t, b: (b, 0, 0)
        tab_specs = [pl.BlockSpec((tm, LANES), lambda t, b: (t, 0)) for _ in tables]
    else:
        grid = (rows // tm, 1)
        row_map = lambda i, _: (i, 0)
        mod_map = lambda i, _: (i // per_mod, 0, 0)
        tab_specs = []
    out_widths = (256, 256, 256, 256, 256, 256, 256, 512, 512, 256)
    return pl.pallas_call(
        functools.partial(_mix_in_kernel, rope=rope),
        grid=grid,
        in_specs=[pl.BlockSpec((tm, d), row_map), pl.BlockSpec((1, N_MOD, d), mod_map),
                  _resident(g_pre.shape)] + [_resident(p.shape) for p in params] + tab_specs,
        out_specs=[pl.BlockSpec((tm, w), row_map) for w in out_widths],
        out_shape=[jax.ShapeDtypeStruct((rows, w), BF16) for w in out_widths],
        compiler_params=_compiler_params(2),
        name="mix_in_rope" if rope else "mix_in",
    )(x, mod, g_pre, *params, *(tables or ()))


def _attn_groups(kind):
    if kind == "a":
        return [(0, [(h * 64 + m * 32, h * 64 + (m + 1) * 32, h) for h in (2 * g, 2 * g + 1)
                     for m in (0, 1)]) for g in range(2)]
    if kind == "b":
        return [(0, [(h * 64, (h + 1) * 64, h) for h in (2 * g, 2 * g + 1)]) for g in range(2)]
    if kind == "d":
        return [(g, [(i * LANES, (i + 1) * LANES, 2 * g + i) for i in range(2)]) for g in range(2)]
    raise ValueError(kind)


ATTN_Q_TILE = dict(a=256, b=512, d=512)
HEAD_W = 64


def _first_chunk_keys(n_keys):
    half = n_keys // 2
    if n_keys >= 4 * MXU_DIM:
        half = -(-half // MXU_DIM) * MXU_DIM
    return half


def _attn_kernel(*refs, groups, n_src, tq, pair, lam_init):
    q_ref = refs[0]
    src = refs[1:1 + 2 * n_src]
    refs = refs[1 + 2 * n_src:]
    if pair:
        lam_ref, gsub_ref = refs[:2]
        refs = refs[2:]
    o_ref, k_scr, vt_scr, s_a, s_b = refs

    off = 0
    for i in range(n_src):
        n = src[2 * i].shape[0]
        k_scr[off:off + n, :] = src[2 * i][...]
        for r in range(0, n, MXU_DIM):
            blk = src[2 * i + 1][r:r + MXU_DIM, :].astype(F32)
            vt_scr[:, off + r:off + r + blk.shape[0]] = blk.T.astype(BF16)
        off += n
    n_keys = off
    half = _first_chunk_keys(n_keys)
    chunk0, chunk1 = slice(0, half), slice(half, n_keys)
    n_tiles = q_ref.shape[0] // tq
    lane_q = lax.broadcasted_iota(jnp.int32, (tq, MXU_DIM), 1)

    if pair:
        lv = lam_ref[...]
        lam = (jnp.exp(jnp.sum(lv[0:1] * lv[1:2], axis=-1, keepdims=True))
               - jnp.exp(jnp.sum(lv[2:3] * lv[3:4], axis=-1, keepdims=True)) + lam_init)

    for slab, maps in groups:
        qcols = slice(slab * MXU_DIM, (slab + 1) * MXU_DIM)
        heads = sorted({h for _, _, h in maps})

        def scores_t(i, k_rows, s_dst):
            qg = q_ref[pl.ds(pl.multiple_of(i * tq, tq), tq), qcols]
            zero = jnp.zeros_like(qg)
            qs = jnp.concatenate([jnp.where((lane_q >= lo) & (lane_q < hi), qg, zero)
                                  for lo, hi, _ in maps], axis=0)
            s_dst[...] = _dot_nt(k_scr[k_rows, qcols], qs)

        def weigh(s, m, k_cols):
            p = jnp.exp2(s - m)
            l = jnp.sum(p, axis=0, keepdims=True)
            pb = p.astype(BF16)
            acc = [_dot(vt_scr[h * HEAD_W:(h + 1) * HEAD_W, k_cols], pb[:, j * tq:(j + 1) * tq])
                   for j, (_, _, h) in enumerate(maps)]
            return l, acc

        scores_t(0, chunk0, s_a)

        def q_tile(i, carry):
            scores_t(i, chunk1, s_b)
            s0 = s_a[...]
            m0 = jnp.max(s0, axis=0, keepdims=True)
            l0, acc0 = weigh(s0, m0, chunk0)
            scores_t(jnp.minimum(i + 1, n_tiles - 1), chunk0, s_a)
            s1 = s_b[...]
            m1 = jnp.maximum(m0, jnp.max(s1, axis=0, keepdims=True))
            alpha = jnp.exp2(m0 - m1)
            l1, acc1 = weigh(s1, m1, chunk1)
            inv_l = 1.0 / (l0 * alpha + l1)

            def out_t(j):
                cols = slice(j * tq, (j + 1) * tq)
                return (acc0[j] * alpha[:, cols] + acc1[j]) * inv_l[:, cols]

            rows = []
            for h in heads:
                js = [j for j, (_, _, hh) in enumerate(maps) if hh == h]
                if pair:
                    o = out_t(js[0]) - lam * out_t(js[1])
                    o = (o * lax.rsqrt(jnp.mean(o * o, axis=0, keepdims=True) + NORM_EPS)
                         * gsub_ref[...] * (1.0 - lam_init))
                else:
                    o = out_t(js[0])
                rows.append(o)
            out = jnp.concatenate(rows, axis=0).T
            ocols = slice(heads[0] * HEAD_W, (heads[-1] + 1) * HEAD_W)
            o_ref[pl.ds(pl.multiple_of(i * tq, tq), tq), ocols] = out.astype(BF16)
            return carry

        lax.fori_loop(0, n_tiles, q_tile, 0)


def _attn(kind, q, sources, extra, *, n_batch, lam_init=0.0):
    groups = _attn_groups(kind)
    pair = kind == "a"
    rows, qw = q.shape
    sq = rows // n_batch
    tq = min(ATTN_Q_TILE[kind], sq)
    n_keys = sum(k.shape[0] // n_batch for k, _ in sources)
    out_w = sources[0][1].shape[1]
    cols = max(len(maps) for _, maps in groups) * tq
    in_specs = [pl.BlockSpec((sq, qw), lambda b: (b, 0))]
    args = [q]
    for k, v in sources:
        n = k.shape[0] // n_batch
        in_specs += [pl.BlockSpec((n, k.shape[1]), lambda b: (b, 0)),
                     pl.BlockSpec((n, v.shape[1]), lambda b: (b, 0))]
        args += [k, v]
    if pair:
        in_specs += [_resident(e.shape) for e in extra]
        args += list(extra)
    assert n_keys % (2 * LANES) == 0 and sq % tq == 0
    half = _first_chunk_keys(n_keys)
    return pl.pallas_call(
        functools.partial(_attn_kernel, groups=groups, n_src=len(sources), tq=tq, pair=pair,
                          lam_init=lam_init),
        grid=(n_batch,),
        in_specs=in_specs,
        out_specs=pl.BlockSpec((sq, out_w), lambda b: (b, 0)),
        out_shape=jax.ShapeDtypeStruct((rows, out_w), BF16),
        scratch_shapes=[pltpu.VMEM((n_keys, qw), BF16), pltpu.VMEM((out_w, n_keys), BF16),
                        pltpu.VMEM((half, cols), F32), pltpu.VMEM((n_keys - half, cols), F32)],
        compiler_params=_compiler_params(1),
        name=f"attn_{kind}_{len(sources)}",
    )(*args)


def _mix_out_kernel(x_ref, a_ref, b_ref, c_ref, d_ref, wout_ref, mod_ref, gpost_ref, o_ref):
    y = _dot(a_ref[...], wout_ref[0])
    y += _dot(b_ref[...], wout_ref[1])
    y += _dot(c_ref[...], wout_ref[2])
    y += _dot(d_ref[...], wout_ref[3])
    o_ref[...] = _post_norm(x_ref[...], y, mod_ref, gpost_ref, 1, 1.0)


def _mix_out(x, groups, w_out, mod, g_post, *, rows_per_mod):
    rows, d = x.shape
    tm = min(ROW_TILE, rows_per_mod)
    per_mod = rows_per_mod // tm
    gw = groups[0].shape[1]
    return pl.pallas_call(
        _mix_out_kernel,
        grid=(rows // tm,),
        in_specs=[pl.BlockSpec((tm, d), lambda i: (i, 0))]
                 + [pl.BlockSpec((tm, gw), lambda i: (i, 0)) for _ in groups]
                 + [_resident(w_out.shape),
                    pl.BlockSpec((1, N_MOD, d), lambda i: (i // per_mod, 0, 0)),
                    _resident(g_post.shape)],
        out_specs=pl.BlockSpec((tm, d), lambda i: (i, 0)),
        out_shape=jax.ShapeDtypeStruct((rows, d), F32),
        compiler_params=_compiler_params(1),
        name="mix_out",
    )(x, *groups, w_out, mod, g_post)


def _rope_table(seq, pattern):
    t = jnp.arange(seq, dtype=jnp.int32)
    pos = (jnp.floor_divide(t, GRID_W).astype(F32), jnp.remainder(t, GRID_W).astype(F32))
    n_freq = next(e[2] for e in pattern if e is not None)
    axis_dim = 2 * n_freq
    inv_freq = ROPE_THETA ** (-jnp.arange(0, axis_dim, 2, dtype=F32) / axis_dim)
    active = np.array([e is not None for e in pattern])
    axis = np.array([e[0] if e is not None else 0 for e in pattern])
    freq = np.array([e[1] if e is not None else 0 for e in pattern])
    first = np.array([bool(e[3]) if e is not None else False for e in pattern])
    ang = jnp.where(axis[None, :] == 0, pos[0][:, None], pos[1][:, None]) * inv_freq[freq][None, :]
    cos = jnp.where(active[None, :], jnp.cos(ang), 1.0)
    sin = jnp.where(active[None, :], jnp.sin(ang), 0.0)
    return cos, jnp.where(first[None, :], -sin, 0.0), jnp.where(first[None, :], 0.0, sin)


def _rope_pattern(rot_dim):
    n_freq = rot_dim // 4
    return [(r // (2 * n_freq), r % n_freq, n_freq, (r % (2 * n_freq)) < n_freq)
            for r in range(rot_dim)]


def _rope_tables(seq):
    p32 = _rope_pattern(32) * (LANES // 32)
    p64 = _rope_pattern(64) * (LANES // 64)
    pd = [None] * 64 + _rope_pattern(32) + [None] * 32
    return _rope_table(seq, p32) + _rope_table(seq, p64) + _rope_table(seq, pd)


def _layer_params(l, w_in, g_qnorm, g_knorm, w_spatial, b_spatial, ln_g, ln_b, g_q_a, w_uq,
                  g_kv_a, w_ukv):
    d_model = w_in.shape[1]
    sizes = (256, 256, 256, 256, 128, 128, 256, 256, 256, 128, 32)
    offs = [0]
    for s in sizes:
        offs.append(offs[-1] + s)
    aq, ak, av, bq, bk, bv, cu, cv, dq, dkv, dkr = (w_in[l][:, offs[i]:offs[i + 1]]
                                                    for i in range(11))
    log2e = math.log2(math.e)
    a_scale = 32 ** -0.5 * log2e
    b_scale = 64 ** -0.5 * log2e
    d_scale = 96 ** -0.5 * log2e

    def rep_kv(w):
        hd = w.shape[1] // B_KV_HEADS
        n_rep = B_HEADS // B_KV_HEADS
        return jnp.concatenate([w[:, (h // n_rep) * hd:(h // n_rep + 1) * hd]
                                for h in range(B_HEADS)], axis=1)

    zeros = lambda n: jnp.zeros((d_model, n), w_in.dtype)
    dkr_placed = jnp.concatenate([zeros(64), dkr, zeros(32)], axis=1)
    blocks = dict(aq=aq * a_scale, ak=ak, av=av, bq=bq, bk=rep_kv(bk), bv=rep_kv(bv), cu=cu, cv=cv,
                  dq=dq, dkr=dkr_placed, dkv=dkv)
    w_ext = jnp.concatenate([blocks[n] for n in _IN_BLOCKS], axis=1).astype(BF16)

    def tile_row(v, n, scale=1.0):
        return (jnp.tile(v, n) * scale).reshape(1, -1)

    bd64 = jnp.kron(jnp.eye(4, dtype=F32), jnp.full((64, 64), 1.0 / 64, F32)).astype(BF16)
    wsp = w_spatial[l].reshape(C_GROUPS * CHUNK, CHUNK).astype(BF16)
    bsp = jnp.repeat(b_spatial[l].T, 256 // C_GROUPS, axis=1)

    uq = w_uq[l].reshape(-1, D_HEADS, 96) * d_scale
    uq = jnp.concatenate([uq, jnp.zeros(uq.shape[:2] + (32,), uq.dtype)], axis=-1)
    wuq = uq.reshape(uq.shape[0], D_HEADS * LANES).astype(BF16)
    ukv = w_ukv[l].reshape(-1, D_HEADS, 128)
    ukv_k = jnp.concatenate([ukv[..., :64], jnp.zeros(ukv.shape[:2] + (64,), ukv.dtype)], axis=-1)
    wukvk = ukv_k.reshape(ukv.shape[0], D_HEADS * LANES).astype(BF16)
    wukvv = ukv[..., 64:].reshape(ukv.shape[0], D_HEADS * 64).astype(BF16)

    return (w_ext, tile_row(g_qnorm[l], B_HEADS, b_scale), tile_row(g_knorm[l], B_HEADS), bd64,
            ln_g[l].reshape(1, -1), ln_b[l].reshape(1, -1), wsp, bsp,
            g_q_a[l].reshape(1, -1), wuq, g_kv_a[l].reshape(1, -1), wukvk, wukvv), bd64


def _ffn_weights(w_in, w_out):
    d, two_ff = w_in.shape
    d_ff = two_ff // 2
    n = d_ff // FF_TILE
    head = d_ff - n * FF_TILE
    if head == 0:
        n, head = n - 1, FF_TILE
    wg, wu = w_in[:, :d_ff].astype(BF16), w_in[:, d_ff:].astype(BF16)
    wd = w_out.astype(BF16)

    def main(w):
        return w[:, head:].reshape(d, n, FF_TILE).transpose(1, 0, 2)

    return (wg[:, :head], wu[:, :head], wd[:head], main(wg), main(wu),
            wd[head:].reshape(n, FF_TILE, d))


def _lambda_init(layer_idx):
    return 0.8 - 0.6 * math.exp(-0.3 * layer_idx)


def kernel(x, c, ctx, c_ctx, w_ada, b_ada, g_pre, g_post, w_ffn1_in, w_ffn1_out, w_ffn2_in,
           w_ffn2_out, w_in, w_out, lam_vecs, g_subln, g_qnorm, g_knorm, w_spatial, b_spatial,
           ln_g, ln_b, g_q_a, w_uq, g_kv_a, w_ukv):
    n_batch, seq, d = x.shape
    n_ctx = ctx.shape[1]
    depth = w_ada.shape[0]

    n_cond = n_batch + 1
    pad = (-n_cond) % 8
    cc = jnp.concatenate([c, c_ctx[None, :], jnp.zeros((pad, d), c.dtype)], axis=0)
    mod = _ada(cc, w_ada, b_ada)
    mod_lat = mod[:, :n_batch].reshape(depth, n_batch, N_MOD, d)
    mod_ctx = mod[:, n_batch:n_cond].reshape(depth, 1, N_MOD, d)

    tables = _rope_tables(seq)
    x_lat = x.reshape(n_batch * seq, d)
    x_ctx = ctx.reshape(n_batch * n_ctx, d)
    lat = dict(rows_per_mod=seq)
    ctxk = dict(rows_per_mod=n_batch * n_ctx)

    for l in range(depth):
        need_ctx = l < depth - 1
        ffn1 = _ffn_weights(w_ffn1_in[l], w_ffn1_out[l])
        ffn2 = _ffn_weights(w_ffn2_in[l], w_ffn2_out[l])
        params, bd64 = _layer_params(l, w_in, g_qnorm, g_knorm, w_spatial, b_spatial, ln_g, ln_b,
                                     g_q_a, w_uq, g_kv_a, w_ukv)
        wo = w_out[l].reshape(N_MIXERS, -1, d).astype(BF16)
        a_extra = (lam_vecs[l], g_subln[l].reshape(-1, 1))
        lam0 = _lambda_init(l)

        x_ctx = _ffn(x_ctx, mod_ctx[l], g_pre[l], g_post[l], ffn1, j=0, **ctxk)
        x_lat = _ffn(x_lat, mod_lat[l], g_pre[l], g_post[l], ffn1, j=0, **lat)

        pc = _mix_in(x_ctx, mod_ctx[l], g_pre[l], params, None, seq=n_ctx, **ctxk)
        pl_ = _mix_in(x_lat, mod_lat[l], g_pre[l], params, tables, seq=seq, **lat)
        qa_c, ka_c, va_c, qb_c, kb_c, vb_c, cc_c, qd_c, kd_c, vd_c = pc
        qa_l, ka_l, va_l, qb_l, kb_l, vb_l, cc_l, qd_l, kd_l, vd_l = pl_

        a_l = _attn("a", qa_l, [(ka_l, va_l), (ka_c, va_c)], a_extra, n_batch=n_batch, lam_init=lam0)
        b_l = _attn("b", qb_l, [(kb_l, vb_l), (kb_c, vb_c)], (), n_batch=n_batch)
        d_l = _attn("d", qd_l, [(kd_l, vd_l), (kd_c, vd_c)], (), n_batch=n_batch)
        x_lat = _mix_out(x_lat, (a_l, b_l, cc_l, d_l), wo, mod_lat[l], g_post[l], **lat)
        x_lat = _ffn(x_lat, mod_lat[l], g_pre[l], g_post[l], ffn2, j=2, **lat)

        if need_ctx:
            a_c = _attn("a", qa_c, [(ka_c, va_c)], a_extra, n_batch=n_batch, lam_init=lam0)
            b_c = _attn("b", qb_c, [(kb_c, vb_c)], (), n_batch=n_batch)
            d_c = _attn("d", qd_c, [(kd_c, vd_c)], (), n_batch=n_batch)
            x_ctx = _mix_out(x_ctx, (a_c, b_c, cc_c, d_c), wo, mod_ctx[l], g_post[l], **ctxk)
            x_ctx = _ffn(x_ctx, mod_ctx[l], g_pre[l], g_post[l], ffn2, j=2, **ctxk)

    return x_lat.reshape(n_batch, seq, d)
```

```python
import functools
import math

import jax
import jax.numpy as jnp
import numpy as np
from jax import lax
from jax.experimental import pallas as pl
from jax.experimental.pallas import tpu as pltpu

F32 = jnp.float32
BF16 = jnp.bfloat16

GRID_W = 64
CHUNK = 128
ROPE_THETA = 10000.0
NORM_EPS = 1e-6
FFN_RES = 0.5
N_MOD = 9
N_MIXERS = 4
A_HEADS, B_HEADS, B_KV_HEADS, C_GROUPS, D_HEADS = 4, 4, 2, 4, 4

LANES = 128
MXU_DIM = 256
VMEM_LIMIT_BYTES = 58 * 2**20

ROW_TILE = 1024
FF_TILE = 768
MIX_SUB_ROWS = 1024


def _dot(a, b):
    return jnp.dot(a, b, preferred_element_type=F32)


def _dot_nt(a, b):
    return lax.dot_general(a, b, (((1,), (1,)), ((), ())), preferred_element_type=F32)


def _split_bf16(x):
    hi = x.astype(BF16)
    lo = (x - hi.astype(F32)).astype(BF16)
    return hi, lo


def _group_mean(xsq, bd_ref):
    hi, lo = _split_bf16(xsq)
    bd = bd_ref[...]
    return _dot(hi, bd) + _dot(lo, bd)


def _rms(x):
    return x * lax.rsqrt(jnp.mean(x * x, axis=-1, keepdims=True) + NORM_EPS)


def _pre_norm(x, mod_ref, gpre_ref, j):
    shift = mod_ref[0, 3 * j:3 * j + 1, :]
    gain = gpre_ref[j:j + 1, :] * (1.0 + mod_ref[0, 3 * j + 1:3 * j + 2, :])
    return _rms(x) * gain + shift


def _post_norm(x, y, mod_ref, gpost_ref, j, res):
    gain = (res * mod_ref[0, 3 * j + 2:3 * j + 3, :]) * gpost_ref[j:j + 1, :]
    return x + _rms(y) * gain


def _gelu_tanh(x):
    c = math.sqrt(2.0 / math.pi)
    return x * (0.5 * (1.0 + jnp.tanh(c * (x + 0.044715 * (x * x * x)))))


def _compiler_params(n_axes):
    return pltpu.CompilerParams(dimension_semantics=("arbitrary",) * n_axes,
                                vmem_limit_bytes=VMEM_LIMIT_BYTES)


def _resident(shape):
    nd = len(shape)
    return pl.BlockSpec(shape, lambda *_: (0,) * nd, pipeline_mode=pl.Buffered(1))


def _ada_kernel(c_ref, w_ref, b_ref, o_ref):
    c = c_ref[...]
    s = c * jax.nn.sigmoid(c)
    s_hi, s_lo = _split_bf16(s)
    w_hi, w_lo = _split_bf16(w_ref[0])
    o_ref[0] = _dot(s_hi, w_hi) + (_dot(s_hi, w_lo) + _dot(s_lo, w_hi)) + b_ref[0]


def _ada(cc, w_ada, b_ada):
    n_layers, d, n = w_ada.shape
    rows = cc.shape[0]
    tn = 1024
    return pl.pallas_call(
        _ada_kernel,
        grid=(n_layers, n // tn),
        in_specs=[pl.BlockSpec((rows, d), lambda l, j: (0, 0)),
                  pl.BlockSpec((1, d, tn), lambda l, j: (l, 0, j)),
                  pl.BlockSpec((1, 1, tn), lambda l, j: (l, 0, j))],
        out_specs=pl.BlockSpec((1, rows, tn), lambda l, j: (l, 0, j)),
        out_shape=jax.ShapeDtypeStruct((n_layers, rows, n), F32),
        compiler_params=_compiler_params(2),
        name="ada",
    )(cc, w_ada, b_ada.reshape(n_layers, 1, n))


def _ffn_kernel(*refs, j, n_mix):
    (x_ref, mod_ref, gpre_ref, gpost_ref, hg_ref, hu_ref, hd_ref, w1g_ref, w1u_ref,
     w2_ref) = refs[:10]
    mix_refs = refs[10:10 + n_mix]
    o_ref, xn_scr, acc_scr = refs[10 + n_mix:]

    def swiglu_chunk(xn, wg, wu, wd):
        hg = _dot(xn, wg)
        hu = _dot(xn, wu)
        return _dot(((hg * jax.nn.sigmoid(hg)) * hu).astype(BF16), wd)

    x = x_ref[...]
    if n_mix:
        wout_ref = mix_refs[-1]
        y = _dot(mix_refs[0][...], wout_ref[0])
        for g in range(1, n_mix - 1):
            y += _dot(mix_refs[g][...], wout_ref[g])
        x = _post_norm(x, y, mod_ref, gpost_ref, 1, 1.0)

    xn = _pre_norm(x, mod_ref, gpre_ref, j).astype(BF16)
    xn_scr[...] = xn
    acc_scr[...] = swiglu_chunk(xn, hg_ref[...], hu_ref[...], hd_ref[...])

    def body(c, carry):
        acc_scr[...] += swiglu_chunk(xn_scr[...], w1g_ref[c], w1u_ref[c], w2_ref[c])
        return carry

    lax.fori_loop(0, w1g_ref.shape[0], body, 0)
    o_ref[...] = _post_norm(x, acc_scr[...], mod_ref, gpost_ref, j, FFN_RES)


def _ffn(x, mod, g_pre, g_post, weights, *, j, rows_per_mod, mixer=None):
    rows, d = x.shape
    tm = min(ROW_TILE, rows_per_mod)
    per_mod = rows_per_mod // tm
    mix_args, mix_specs = [], []
    if mixer is not None:
        groups, w_out = mixer
        mix_args = list(groups) + [w_out]
        mix_specs = ([pl.BlockSpec((tm, g.shape[1]), lambda i: (i, 0)) for g in groups]
                     + [_resident(w_out.shape)])
    return pl.pallas_call(
        functools.partial(_ffn_kernel, j=j, n_mix=len(mix_args)),
        grid=(rows // tm,),
        in_specs=[pl.BlockSpec((tm, d), lambda i: (i, 0)),
                  pl.BlockSpec((1, N_MOD, d), lambda i: (i // per_mod, 0, 0)),
                  _resident(g_pre.shape), _resident(g_post.shape)]
                 + [_resident(w.shape) for w in weights] + mix_specs,
        out_specs=pl.BlockSpec((tm, d), lambda i: (i, 0)),
        out_shape=jax.ShapeDtypeStruct((rows, d), F32),
        scratch_shapes=[pltpu.VMEM((tm, d), BF16), pltpu.VMEM((tm, d), F32)],
        compiler_params=_compiler_params(1),
        name=f"ffn{j}" if mixer is None else f"mix_out_ffn{j}",
    )(x, mod, g_pre, g_post, *weights, *mix_args)


_IN_BLOCKS = ("aq", "ak", "av", "bq", "bk", "bv", "cu", "cv", "dq", "dkr", "dkv")
_IN_WIDTHS = dict(aq=256, ak=256, av=256, bq=256, bk=256, bv=256, cu=256, cv=256,
                  dq=256, dkr=128, dkv=128)
_IN_OFFSETS = {}
_off = 0
for _n in _IN_BLOCKS:
    _IN_OFFSETS[_n] = _off
    _off += _IN_WIDTHS[_n]
IN_EXT_WIDTH = _off


def _rope_slabs(x, tabs, shift, store):
    cos, s_up, s_dn = tabs
    for s in range(x.shape[1] // LANES):
        xs = x[:, s * LANES:(s + 1) * LANES]
        if cos is not None:
            xs = (xs * cos + pltpu.roll(xs, LANES - shift, axis=1) * s_up
                  + pltpu.roll(xs, shift, axis=1) * s_dn)
        store(s, xs.astype(BF16))


def _mix_in_kernel(*refs, rope):
    (x_ref, mod_ref, gpre_ref, win_ref, gq_ref, gk_ref, bd64_ref, lng_ref, lnb_ref, wsp_ref,
     bsp_ref, gqa_ref, wuq_ref, gkva_ref, wukvk_ref, wukvv_ref) = refs[:16]
    refs = refs[16:]
    tab_refs = refs[:9] if rope else ()
    refs = refs[9:] if rope else refs
    qa_ref, ka_ref, va_ref, qb_ref, kb_ref, vb_ref, cc_ref, qd_ref, kd_ref, vd_ref = refs
    sub = min(MIX_SUB_ROWS, x_ref.shape[0])

    def sub_block(r, carry):
        rows = pl.ds(pl.multiple_of(r * sub, sub), sub)
        if rope:
            tabs = [t[rows, :] for t in tab_refs]
            t32, t64, td = tabs[0:3], tabs[3:6], tabs[6:9]
        else:
            t32 = t64 = td = (None, None, None)

        xn = _pre_norm(x_ref[rows, :], mod_ref, gpre_ref, 1).astype(BF16)

        def proj(name):
            o = _IN_OFFSETS[name]
            return _dot(xn, win_ref[:, o:o + _IN_WIDTHS[name]])

        def slab_store(ref):
            def store(s, v):
                ref[rows, s * LANES:(s + 1) * LANES] = v
            return store

        _rope_slabs(proj("aq"), t32, 8, slab_store(qa_ref))
        _rope_slabs(proj("ak"), t32, 8, slab_store(ka_ref))
        va_ref[rows, :] = proj("av").astype(BF16)

        def qk_norm(p, g_ref):
            return p * lax.rsqrt(_group_mean(p * p, bd64_ref) + NORM_EPS) * g_ref[...]

        _rope_slabs(qk_norm(proj("bq"), gq_ref), t64, 16, slab_store(qb_ref))
        _rope_slabs(qk_norm(proj("bk"), gk_ref), t64, 16, slab_store(kb_ref))
        vb_ref[rows, :] = proj("bv").astype(BF16)

        u = _gelu_tanh(proj("cu"))
        v = _gelu_tanh(proj("cv"))
        mu = jnp.mean(v, axis=-1, keepdims=True)
        vc = v - mu
        vln = (vc * lax.rsqrt(jnp.mean(vc * vc, axis=-1, keepdims=True) + NORM_EPS) * lng_ref[...]
               + lnb_ref[...]).astype(BF16)
        width = vln.shape[1]
        gw = width // C_GROUPS
        lane = lax.broadcasted_iota(jnp.int32, (CHUNK, width), 1)
        wsp = wsp_ref[...]
        bsp = bsp_ref[...]
        gated = []
        for ci in range(sub // CHUNK):
            crows = slice(ci * CHUNK, (ci + 1) * CHUNK)
            allg = _dot(wsp, vln[crows, :])
            mixed = allg[0:CHUNK]
            for g in range(1, C_GROUPS):
                mixed = jnp.where(lane >= g * gw, allg[g * CHUNK:(g + 1) * CHUNK], mixed)
            gated.append((u[crows, :] * (mixed + bsp)).astype(BF16))
        cc_ref[rows, :] = jnp.concatenate(gated, axis=0)

        cq = (_rms(proj("dq")) * gqa_ref[...]).astype(BF16)
        _rope_slabs(_dot(cq, wuq_ref[...]), td, 8, slab_store(qd_ref))
        ckv = (_rms(proj("dkv")) * gkva_ref[...]).astype(BF16)
        kr = proj("dkr")
        _rope_slabs(_dot(ckv, wukvk_ref[...]) + jnp.concatenate([kr] * D_HEADS, axis=1), td, 8,
                    slab_store(kd_ref))
        vd_ref[rows, :] = _dot(ckv, wukvv_ref[...]).astype(BF16)
        return carry

    lax.fori_loop(0, x_ref.shape[0] // sub, sub_block, 0)


def _mix_in(x, mod, g_pre, params, tables, *, rows_per_mod, seq):
    rows, d = x.shape
    tm = min(ROW_TILE, rows_per_mod)
    per_mod = rows_per_mod // tm
    rope = tables is not None
    n_mod_blocks = rows // rows_per_mod
    if rope:
        grid = (seq // tm, n_mod_blocks)
        row_map = lambda t, b: (b * per_mod + t, 0)
        mod_map = lambda t, b: (b, 0, 0)
        tab_specs = [pl.BlockSpec((tm, LANES), lambda t, b: (t, 0)) for _ in tables]
    else:
        grid = (rows // tm, 1)
        row_map = lambda i, _: (i, 0)
        mod_map = lambda i, _: (i // per_mod, 0, 0)
        tab_specs = []
    out_widths = (256, 256, 256, 256, 256, 256, 256, 512, 512, 256)
    return pl.pallas_call(
        functools.partial(_mix_in_kernel, rope=rope),
        grid=grid,
        in_specs=[pl.BlockSpec((tm, d), row_map), pl.BlockSpec((1, N_MOD, d), mod_map),
                  _resident(g_pre.shape)] + [_resident(p.shape) for p in params] + tab_specs,
        out_specs=[pl.BlockSpec((tm, w), row_map) for w in out_widths],
        out_shape=[jax.ShapeDtypeStruct((rows, w), BF16) for w in out_widths],
        compiler_params=_compiler_params(2),
        name="mix_in_rope" if rope else "mix_in",
    )(x, mod, g_pre, *params, *(tables or ()))


def _attn_groups(kind):
    if kind == "a":
        return [(0, [(h * 64 + m * 32, h * 64 + (m + 1) * 32, h) for h in (2 * g, 2 * g + 1)
                     for m in (0, 1)]) for g in range(2)]
    if kind == "b":
        return [(0, [(h * 64, (h + 1) * 64, h) for h in (2 * g, 2 * g + 1)]) for g in range(2)]
    if kind == "d":
        return [(g, [(i * LANES, (i + 1) * LANES, 2 * g + i) for i in range(2)]) for g in range(2)]
    raise ValueError(kind)


ATTN_Q_TILE = dict(a=256, b=512, d=256)
HEAD_W = 64


def _first_chunk_keys(n_keys):
    half = n_keys // 2
    if n_keys >= 4 * MXU_DIM:
        half = -(-half // MXU_DIM) * MXU_DIM
    return half


def _attn_kernel(*refs, groups, n_src, tq, pair, lam_init):
    q_ref = refs[0]
    src = refs[1:1 + 2 * n_src]
    refs = refs[1 + 2 * n_src:]
    if pair:
        lam_ref, gsub_ref = refs[:2]
        refs = refs[2:]
    o_ref, k_scr, vt_scr, s_a, s_b = refs

    off = 0
    for i in range(n_src):
        n = src[2 * i].shape[0]
        k_scr[off:off + n, :] = src[2 * i][...]
        for r in range(0, n, MXU_DIM):
            blk = src[2 * i + 1][r:r + MXU_DIM, :].astype(F32)
            vt_scr[:, off + r:off + r + blk.shape[0]] = blk.T.astype(BF16)
        off += n
    n_keys = off
    half = _first_chunk_keys(n_keys)
    chunk0, chunk1 = slice(0, half), slice(half, n_keys)
    n_tiles = q_ref.shape[0] // tq
    lane_q = lax.broadcasted_iota(jnp.int32, (tq, MXU_DIM), 1)

    if pair:
        lv = lam_ref[...]
        lam = (jnp.exp(jnp.sum(lv[0:1] * lv[1:2], axis=-1, keepdims=True))
               - jnp.exp(jnp.sum(lv[2:3] * lv[3:4], axis=-1, keepdims=True)) + lam_init)

    def tile_rows(i):
        if isinstance(i, int):
            return slice(i * tq, (i + 1) * tq)
        return pl.ds(pl.multiple_of(i * tq, tq), tq)

    def scores_t(group, i, k_rows, s_dst):
        slab, maps = group
        qcols = slice(slab * MXU_DIM, (slab + 1) * MXU_DIM)
        qg = q_ref[tile_rows(i), qcols]
        zero = jnp.zeros_like(qg)
        qs = jnp.concatenate([jnp.where((lane_q >= lo) & (lane_q < hi), qg, zero)
                              for lo, hi, _ in maps], axis=0)
        s_dst[...] = _dot_nt(k_scr[k_rows, qcols], qs)

    def q_tile(group, i, prefetch):
        _, maps = group
        heads = sorted({h for _, _, h in maps})

        def weigh(s, m, k_cols):
            p = jnp.exp2(s - m)
            l = jnp.sum(p, axis=0, keepdims=True)
            pb = p.astype(BF16)
            acc = [_dot(vt_scr[h * HEAD_W:(h + 1) * HEAD_W, k_cols], pb[:, j * tq:(j + 1) * tq])
                   for j, (_, _, h) in enumerate(maps)]
            return l, acc

        scores_t(group, i, chunk1, s_b)
        s0 = s_a[...]
        m0 = jnp.max(s0, axis=0, keepdims=True)
        l0, acc0 = weigh(s0, m0, chunk0)
        if prefetch is not None:
            prefetch()
        s1 = s_b[...]
        m1 = jnp.maximum(m0, jnp.max(s1, axis=0, keepdims=True))
        alpha = jnp.exp2(m0 - m1)
        l1, acc1 = weigh(s1, m1, chunk1)
        inv_l = 1.0 / (l0 * alpha + l1)

        def out_t(j):
            cols = slice(j * tq, (j + 1) * tq)
            return (acc0[j] * alpha[:, cols] + acc1[j]) * inv_l[:, cols]

        rows = []
        for h in heads:
            js = [j for j, (_, _, hh) in enumerate(maps) if hh == h]
            if pair:
                o = out_t(js[0]) - lam * out_t(js[1])
                o = (o * lax.rsqrt(jnp.mean(o * o, axis=0, keepdims=True) + NORM_EPS)
                     * gsub_ref[...] * (1.0 - lam_init))
            else:
                o = out_t(js[0])
            rows.append(o)
        out = jnp.concatenate(rows, axis=0).T
        ocols = slice(heads[0] * HEAD_W, (heads[-1] + 1) * HEAD_W)
        o_ref[tile_rows(i), ocols] = out.astype(BF16)

    scores_t(groups[0], 0, chunk0, s_a)
    for gi, group in enumerate(groups):
        def body(i, carry, group=group):
            q_tile(group, i, lambda: scores_t(group, i + 1, chunk0, s_a))
            return carry

        if n_tiles > 1:
            lax.fori_loop(0, n_tiles - 1, body, 0)
        if gi + 1 < len(groups):
            q_tile(group, n_tiles - 1,
                   lambda nxt=groups[gi + 1]: scores_t(nxt, 0, chunk0, s_a))
        else:
            q_tile(group, n_tiles - 1, None)


def _attn(kind, q, sources, extra, *, n_batch, lam_init=0.0):
    groups = _attn_groups(kind)
    pair = kind == "a"
    rows, qw = q.shape
    sq = rows // n_batch
    tq = min(ATTN_Q_TILE[kind], sq)
    n_keys = sum(k.shape[0] // n_batch for k, _ in sources)
    out_w = sources[0][1].shape[1]
    cols = max(len(maps) for _, maps in groups) * tq
    in_specs = [pl.BlockSpec((sq, qw), lambda b: (b, 0))]
    args = [q]
    for k, v in sources:
        n = k.shape[0] // n_batch
        in_specs += [pl.BlockSpec((n, k.shape[1]), lambda b: (b, 0)),
                     pl.BlockSpec((n, v.shape[1]), lambda b: (b, 0))]
        args += [k, v]
    if pair:
        in_specs += [_resident(e.shape) for e in extra]
        args += list(extra)
    assert n_keys % (2 * LANES) == 0 and sq % tq == 0
    half = _first_chunk_keys(n_keys)
    return pl.pallas_call(
        functools.partial(_attn_kernel, groups=groups, n_src=len(sources), tq=tq, pair=pair,
                          lam_init=lam_init),
        grid=(n_batch,),
        in_specs=in_specs,
        out_specs=pl.BlockSpec((sq, out_w), lambda b: (b, 0)),
        out_shape=jax.ShapeDtypeStruct((rows, out_w), BF16),
        scratch_shapes=[pltpu.VMEM((n_keys, qw), BF16), pltpu.VMEM((out_w, n_keys), BF16),
                        pltpu.VMEM((half, cols), F32), pltpu.VMEM((n_keys - half, cols), F32)],
        compiler_params=_compiler_params(1),
        name=f"attn_{kind}_{len(sources)}",
    )(*args)


def _rope_table(seq, pattern):
    t = jnp.arange(seq, dtype=jnp.int32)
    pos = (jnp.floor_divide(t, GRID_W).astype(F32), jnp.remainder(t, GRID_W).astype(F32))
    n_freq = next(e[2] for e in pattern if e is not None)
    axis_dim = 2 * n_freq
    inv_freq = ROPE_THETA ** (-jnp.arange(0, axis_dim, 2, dtype=F32) / axis_dim)
    active = np.array([e is not None for e in pattern])
    axis = np.array([e[0] if e is not None else 0 for e in pattern])
    freq = np.array([e[1] if e is not None else 0 for e in pattern])
    first = np.array([bool(e[3]) if e is not None else False for e in pattern])
    ang = jnp.where(axis[None, :] == 0, pos[0][:, None], pos[1][:, None]) * inv_freq[freq][None, :]
    cos = jnp.where(active[None, :], jnp.cos(ang), 1.0)
    sin = jnp.where(active[None, :], jnp.sin(ang), 0.0)
    return cos, jnp.where(first[None, :], -sin, 0.0), jnp.where(first[None, :], 0.0, sin)


def _rope_pattern(rot_dim):
    n_freq = rot_dim // 4
    return [(r // (2 * n_freq), r % n_freq, n_freq, (r % (2 * n_freq)) < n_freq)
            for r in range(rot_dim)]


def _rope_tables(seq):
    p32 = _rope_pattern(32) * (LANES // 32)
    p64 = _rope_pattern(64) * (LANES // 64)
    pd = [None] * 64 + _rope_pattern(32) + [None] * 32
    return _rope_table(seq, p32) + _rope_table(seq, p64) + _rope_table(seq, pd)


def _layer_params(l, w_in, g_qnorm, g_knorm, w_spatial, b_spatial, ln_g, ln_b, g_q_a, w_uq,
                  g_kv_a, w_ukv):
    d_model = w_in.shape[1]
    sizes = (256, 256, 256, 256, 128, 128, 256, 256, 256, 128, 32)
    offs = [0]
    for s in sizes:
        offs.append(offs[-1] + s)
    aq, ak, av, bq, bk, bv, cu, cv, dq, dkv, dkr = (w_in[l][:, offs[i]:offs[i + 1]]
                                                    for i in range(11))
    log2e = math.log2(math.e)
    a_scale = 32 ** -0.5 * log2e
    b_scale = 64 ** -0.5 * log2e
    d_scale = 96 ** -0.5 * log2e

    def rep_kv(w):
        hd = w.shape[1] // B_KV_HEADS
        n_rep = B_HEADS // B_KV_HEADS
        return jnp.concatenate([w[:, (h // n_rep) * hd:(h // n_rep + 1) * hd]
                                for h in range(B_HEADS)], axis=1)

    zeros = lambda n: jnp.zeros((d_model, n), w_in.dtype)
    dkr_placed = jnp.concatenate([zeros(64), dkr, zeros(32)], axis=1)
    blocks = dict(aq=aq * a_scale, ak=ak, av=av, bq=bq, bk=rep_kv(bk), bv=rep_kv(bv), cu=cu, cv=cv,
                  dq=dq, dkr=dkr_placed, dkv=dkv)
    w_ext = jnp.concatenate([blocks[n] for n in _IN_BLOCKS], axis=1).astype(BF16)

    def tile_row(v, n, scale=1.0):
        return (jnp.tile(v, n) * scale).reshape(1, -1)

    bd64 = jnp.kron(jnp.eye(4, dtype=F32), jnp.full((64, 64), 1.0 / 64, F32)).astype(BF16)
    wsp = w_spatial[l].reshape(C_GROUPS * CHUNK, CHUNK).astype(BF16)
    bsp = jnp.repeat(b_spatial[l].T, 256 // C_GROUPS, axis=1)

    uq = w_uq[l].reshape(-1, D_HEADS, 96) * d_scale
    uq = jnp.concatenate([uq, jnp.zeros(uq.shape[:2] + (32,), uq.dtype)], axis=-1)
    wuq = uq.reshape(uq.shape[0], D_HEADS * LANES).astype(BF16)
    ukv = w_ukv[l].reshape(-1, D_HEADS, 128)
    ukv_k = jnp.concatenate([ukv[..., :64], jnp.zeros(ukv.shape[:2] + (64,), ukv.dtype)], axis=-1)
    wukvk = ukv_k.reshape(ukv.shape[0], D_HEADS * LANES).astype(BF16)
    wukvv = ukv[..., 64:].reshape(ukv.shape[0], D_HEADS * 64).astype(BF16)

    return (w_ext, tile_row(g_qnorm[l], B_HEADS, b_scale), tile_row(g_knorm[l], B_HEADS), bd64,
            ln_g[l].reshape(1, -1), ln_b[l].reshape(1, -1), wsp, bsp,
            g_q_a[l].reshape(1, -1), wuq, g_kv_a[l].reshape(1, -1), wukvk, wukvv), bd64


def _ffn_weights(w_in, w_out):
    d, two_ff = w_in.shape
    d_ff = two_ff // 2
    n = d_ff // FF_TILE
    head = d_ff - n * FF_TILE
    if head == 0:
        n, head = n - 1, FF_TILE
    wg, wu = w_in[:, :d_ff].astype(BF16), w_in[:, d_ff:].astype(BF16)
    wd = w_out.astype(BF16)

    def main(w):
        return w[:, head:].reshape(d, n, FF_TILE).transpose(1, 0, 2)

    return (wg[:, :head], wu[:, :head], wd[:head], main(wg), main(wu),
            wd[head:].reshape(n, FF_TILE, d))


def _lambda_init(layer_idx):
    return 0.8 - 0.6 * math.exp(-0.3 * layer_idx)


def kernel(x, c, ctx, c_ctx, w_ada, b_ada, g_pre, g_post, w_ffn1_in, w_ffn1_out, w_ffn2_in,
           w_ffn2_out, w_in, w_out, lam_vecs, g_subln, g_qnorm, g_knorm, w_spatial, b_spatial,
           ln_g, ln_b, g_q_a, w_uq, g_kv_a, w_ukv):
    n_batch, seq, d = x.shape
    n_ctx = ctx.shape[1]
    depth = w_ada.shape[0]

    n_cond = n_batch + 1
    pad = (-n_cond) % 8
    cc = jnp.concatenate([c, c_ctx[None, :], jnp.zeros((pad, d), c.dtype)], axis=0)
    mod = _ada(cc, w_ada, b_ada)
    mod_lat = mod[:, :n_batch].reshape(depth, n_batch, N_MOD, d)
    mod_ctx = mod[:, n_batch:n_cond].reshape(depth, 1, N_MOD, d)

    tables = _rope_tables(seq)
    x_lat = x.reshape(n_batch * seq, d)
    x_ctx = ctx.reshape(n_batch * n_ctx, d)
    lat = dict(rows_per_mod=seq)
    ctxk = dict(rows_per_mod=n_batch * n_ctx)

    for l in range(depth):
        need_ctx = l < depth - 1
        ffn1 = _ffn_weights(w_ffn1_in[l], w_ffn1_out[l])
        ffn2 = _ffn_weights(w_ffn2_in[l], w_ffn2_out[l])
        params, bd64 = _layer_params(l, w_in, g_qnorm, g_knorm, w_spatial, b_spatial, ln_g, ln_b,
                                     g_q_a, w_uq, g_kv_a, w_ukv)
        wo = w_out[l].reshape(N_MIXERS, -1, d).astype(BF16)
        a_extra = (lam_vecs[l], g_subln[l].reshape(-1, 1))
        lam0 = _lambda_init(l)

        x_ctx = _ffn(x_ctx, mod_ctx[l], g_pre[l], g_post[l], ffn1, j=0, **ctxk)
        x_lat = _ffn(x_lat, mod_lat[l], g_pre[l], g_post[l], ffn1, j=0, **lat)

        pc = _mix_in(x_ctx, mod_ctx[l], g_pre[l], params, None, seq=n_ctx, **ctxk)
        pl_ = _mix_in(x_lat, mod_lat[l], g_pre[l], params, tables, seq=seq, **lat)
        qa_c, ka_c, va_c, qb_c, kb_c, vb_c, cc_c, qd_c, kd_c, vd_c = pc
        qa_l, ka_l, va_l, qb_l, kb_l, vb_l, cc_l, qd_l, kd_l, vd_l = pl_

        a_l = _attn("a", qa_l, [(ka_l, va_l), (ka_c, va_c)], a_extra, n_batch=n_batch, lam_init=lam0)
        b_l = _attn("b", qb_l, [(kb_l, vb_l), (kb_c, vb_c)], (), n_batch=n_batch)
        d_l = _attn("d", qd_l, [(kd_l, vd_l), (kd_c, vd_c)], (), n_batch=n_batch)
        x_lat = _ffn(x_lat, mod_lat[l], g_pre[l], g_post[l], ffn2, j=2,
                     mixer=((a_l, b_l, cc_l, d_l), wo), **lat)

        if need_ctx:
            a_c = _attn("a", qa_c, [(ka_c, va_c)], a_extra, n_batch=n_batch, lam_init=lam0)
            b_c = _attn("b", qb_c, [(kb_c, vb_c)], (), n_batch=n_batch)
            d_c = _attn("d", qd_c, [(kd_c, vd_c)], (), n_batch=n_batch)
            x_ctx = _ffn(x_ctx, mod_ctx[l], g_pre[l], g_post[l], ffn2, j=2,
                         mixer=((a_c, b_c, cc_c, d_c), wo), **ctxk)

    return x_lat.reshape(n_batch, seq, d)
```

```python
import functools
import math

import jax
import jax.numpy as jnp
import numpy as np
from jax import lax
from jax.experimental import pallas as pl
from jax.experimental.pallas import tpu as pltpu

F32 = jnp.float32
BF16 = jnp.bfloat16

GRID_W = 64
CHUNK = 128
ROPE_THETA = 10000.0
NORM_EPS = 1e-6
FFN_RES = 0.5
N_MOD = 9
N_MIXERS = 4
A_HEADS, B_HEADS, B_KV_HEADS, C_GROUPS, D_HEADS = 4, 4, 2, 4, 4

LANES = 128
MXU_DIM = 256
VMEM_LIMIT_BYTES = 58 * 2**20

ROW_TILE = 1024
FF_TILE = 768
MIX_SUB_ROWS = 1024


def _dot(a, b):
    return jnp.dot(a, b, preferred_element_type=F32)


def _dot_nt(a, b):
    return lax.dot_general(a, b, (((1,), (1,)), ((), ())), preferred_element_type=F32)


def _split_bf16(x):
    hi = x.astype(BF16)
    lo = (x - hi.astype(F32)).astype(BF16)
    return hi, lo


def _group_mean(xsq, bd_ref):
    hi, lo = _split_bf16(xsq)
    bd = bd_ref[...]
    return _dot(hi, bd) + _dot(lo, bd)


def _rms(x):
    return x * lax.rsqrt(jnp.mean(x * x, axis=-1, keepdims=True) + NORM_EPS)


def _pre_norm(x, mod_ref, gpre_ref, j):
    shift = mod_ref[0, 3 * j:3 * j + 1, :]
    gain = gpre_ref[j:j + 1, :] * (1.0 + mod_ref[0, 3 * j + 1:3 * j + 2, :])
    return _rms(x) * gain + shift


def _post_norm(x, y, mod_ref, gpost_ref, j, res):
    gain = (res * mod_ref[0, 3 * j + 2:3 * j + 3, :]) * gpost_ref[j:j + 1, :]
    return x + _rms(y) * gain


def _gelu_tanh(x):
    c = math.sqrt(2.0 / math.pi)
    return x * (0.5 * (1.0 + jnp.tanh(c * (x + 0.044715 * (x * x * x)))))


def _compiler_params(n_axes):
    return pltpu.CompilerParams(dimension_semantics=("arbitrary",) * n_axes,
                                vmem_limit_bytes=VMEM_LIMIT_BYTES)


def _resident(shape):
    nd = len(shape)
    return pl.BlockSpec(shape, lambda *_: (0,) * nd, pipeline_mode=pl.Buffered(1))


def _ada_kernel(c_ref, w_ref, b_ref, o_ref):
    c = c_ref[...]
    s = c * jax.nn.sigmoid(c)
    s_hi, s_lo = _split_bf16(s)
    w_hi, w_lo = _split_bf16(w_ref[0])
    o_ref[0] = _dot(s_hi, w_hi) + (_dot(s_hi, w_lo) + _dot(s_lo, w_hi)) + b_ref[0]


def _ada(cc, w_ada, b_ada):
    n_layers, d, n = w_ada.shape
    rows = cc.shape[0]
    tn = 1024
    return pl.pallas_call(
        _ada_kernel,
        grid=(n_layers, n // tn),
        in_specs=[pl.BlockSpec((rows, d), lambda l, j: (0, 0)),
                  pl.BlockSpec((1, d, tn), lambda l, j: (l, 0, j)),
                  pl.BlockSpec((1, 1, tn), lambda l, j: (l, 0, j))],
        out_specs=pl.BlockSpec((1, rows, tn), lambda l, j: (l, 0, j)),
        out_shape=jax.ShapeDtypeStruct((n_layers, rows, n), F32),
        compiler_params=_compiler_params(2),
        name="ada",
    )(cc, w_ada, b_ada.reshape(n_layers, 1, n))


def _ffn_kernel(*refs, j, n_mix):
    (x_ref, mod_ref, gpre_ref, gpost_ref, hg_ref, hu_ref, hd_ref, w1g_ref, w1u_ref,
     w2_ref) = refs[:10]
    mix_refs = refs[10:10 + n_mix]
    o_ref, xn_scr, acc_scr = refs[10 + n_mix:]

    def swiglu_chunk(xn, wg, wu, wd):
        hg = _dot(xn, wg)
        hu = _dot(xn, wu)
        return _dot(((hg * jax.nn.sigmoid(hg)) * hu).astype(BF16), wd)

    x = x_ref[...]
    if n_mix:
        wout_ref = mix_refs[-1]
        y = _dot(mix_refs[0][...], wout_ref[0])
        for g in range(1, n_mix - 1):
            y += _dot(mix_refs[g][...], wout_ref[g])
        x = _post_norm(x, y, mod_ref, gpost_ref, 1, 1.0)

    xn = _pre_norm(x, mod_ref, gpre_ref, j).astype(BF16)
    xn_scr[...] = xn
    acc_scr[...] = swiglu_chunk(xn, hg_ref[...], hu_ref[...], hd_ref[...])

    def body(c, carry):
        acc_scr[...] += swiglu_chunk(xn_scr[...], w1g_ref[c], w1u_ref[c], w2_ref[c])
        return carry

    lax.fori_loop(0, w1g_ref.shape[0], body, 0)
    o_ref[...] = _post_norm(x, acc_scr[...], mod_ref, gpost_ref, j, FFN_RES)


def _ffn(x, mod, g_pre, g_post, weights, *, j, rows_per_mod, mixer=None):
    rows, d = x.shape
    tm = min(ROW_TILE, rows_per_mod)
    per_mod = rows_per_mod // tm
    mix_args, mix_specs = [], []
    if mixer is not None:
        groups, w_out = mixer
        mix_args = list(groups) + [w_out]
        mix_specs = ([pl.BlockSpec((tm, g.shape[1]), lambda i: (i, 0)) for g in groups]
                     + [_resident(w_out.shape)])
    return pl.pallas_call(
        functools.partial(_ffn_kernel, j=j, n_mix=len(mix_args)),
        grid=(rows // tm,),
        in_specs=[pl.BlockSpec((tm, d), lambda i: (i, 0)),
                  pl.BlockSpec((1, N_MOD, d), lambda i: (i // per_mod, 0, 0)),
                  _resident(g_pre.shape), _resident(g_post.shape)]
                 + [_resident(w.shape) for w in weights] + mix_specs,
        out_specs=pl.BlockSpec((tm, d), lambda i: (i, 0)),
        out_shape=jax.ShapeDtypeStruct((rows, d), F32),
        scratch_shapes=[pltpu.VMEM((tm, d), BF16), pltpu.VMEM((tm, d), F32)],
        compiler_params=_compiler_params(1),
        name=f"ffn{j}" if mixer is None else f"mix_out_ffn{j}",
    )(x, mod, g_pre, g_post, *weights, *mix_args)


_IN_BLOCKS = ("aq", "ak", "av", "bq", "bk", "bv", "cu", "cv", "dq", "dkr", "dkv")
_IN_WIDTHS = dict(aq=256, ak=256, av=256, bq=256, bk=256, bv=256, cu=256, cv=256,
                  dq=256, dkr=128, dkv=128)
_IN_OFFSETS = {}
_off = 0
for _n in _IN_BLOCKS:
    _IN_OFFSETS[_n] = _off
    _off += _IN_WIDTHS[_n]
IN_EXT_WIDTH = _off


def _rope_slabs(x, tabs, shift, store):
    cos, sin = tabs
    if cos is not None:
        lane = lax.broadcasted_iota(jnp.int32, cos.shape, 1)
        first = (lane & (2 * shift - 1)) < shift
    for s in range(x.shape[1] // LANES):
        xs = x[:, s * LANES:(s + 1) * LANES]
        if cos is not None:
            partner = jnp.where(first, pltpu.roll(xs, LANES - shift, axis=1),
                                pltpu.roll(xs, shift, axis=1))
            xs = xs * cos + partner * sin
        store(s, xs.astype(BF16))


def _mix_in_kernel(*refs, rope):
    (x_ref, mod_ref, gpre_ref, win_ref, gq_ref, gk_ref, bd64_ref, lng_ref, lnb_ref, wsp_ref,
     bsp_ref, gqa_ref, wuq_ref, gkva_ref, wukvk_ref, wukvv_ref) = refs[:16]
    refs = refs[16:]
    tab_refs = refs[:6] if rope else ()
    refs = refs[6:] if rope else refs
    qa_ref, ka_ref, va_ref, qb_ref, kb_ref, vb_ref, cc_ref, qd_ref, kd_ref, vd_ref = refs
    sub = min(MIX_SUB_ROWS, x_ref.shape[0])

    def sub_block(r, carry):
        rows = pl.ds(pl.multiple_of(r * sub, sub), sub)
        if rope:
            tabs = [t[rows, :] for t in tab_refs]
            t32, t64, td = tabs[0:2], tabs[2:4], tabs[4:6]
        else:
            t32 = t64 = td = (None, None)

        xn = _pre_norm(x_ref[rows, :], mod_ref, gpre_ref, 1).astype(BF16)

        def proj(name):
            o = _IN_OFFSETS[name]
            return _dot(xn, win_ref[:, o:o + _IN_WIDTHS[name]])

        def slab_store(ref):
            def store(s, v):
                ref[rows, s * LANES:(s + 1) * LANES] = v
            return store

        _rope_slabs(proj("aq"), t32, 8, slab_store(qa_ref))
        _rope_slabs(proj("ak"), t32, 8, slab_store(ka_ref))
        va_ref[rows, :] = proj("av").astype(BF16)

        def qk_norm(p, g_ref):
            return p * lax.rsqrt(_group_mean(p * p, bd64_ref) + NORM_EPS) * g_ref[...]

        _rope_slabs(qk_norm(proj("bq"), gq_ref), t64, 16, slab_store(qb_ref))
        _rope_slabs(qk_norm(proj("bk"), gk_ref), t64, 16, slab_store(kb_ref))
        vb_ref[rows, :] = proj("bv").astype(BF16)

        u = _gelu_tanh(proj("cu"))
        v = _gelu_tanh(proj("cv"))
        mu = jnp.mean(v, axis=-1, keepdims=True)
        vc = v - mu
        vln = (vc * lax.rsqrt(jnp.mean(vc * vc, axis=-1, keepdims=True) + NORM_EPS) * lng_ref[...]
               + lnb_ref[...]).astype(BF16)
        width = vln.shape[1]
        gw = width // C_GROUPS
        lane = lax.broadcasted_iota(jnp.int32, (CHUNK, width), 1)
        wsp = wsp_ref[...]
        bsp = bsp_ref[...]
        gated = []
        for ci in range(sub // CHUNK):
            crows = slice(ci * CHUNK, (ci + 1) * CHUNK)
            allg = _dot(wsp, vln[crows, :])
            mixed = allg[0:CHUNK]
            for g in range(1, C_GROUPS):
                mixed = jnp.where(lane >= g * gw, allg[g * CHUNK:(g + 1) * CHUNK], mixed)
            gated.append((u[crows, :] * (mixed + bsp)).astype(BF16))
        cc_ref[rows, :] = jnp.concatenate(gated, axis=0)

        cq = (_rms(proj("dq")) * gqa_ref[...]).astype(BF16)
        _rope_slabs(_dot(cq, wuq_ref[...]), td, 8, slab_store(qd_ref))
        ckv = (_rms(proj("dkv")) * gkva_ref[...]).astype(BF16)
        kr = proj("dkr")
        _rope_slabs(_dot(ckv, wukvk_ref[...]) + jnp.concatenate([kr] * D_HEADS, axis=1), td, 8,
                    slab_store(kd_ref))
        vd_ref[rows, :] = _dot(ckv, wukvv_ref[...]).astype(BF16)
        return carry

    lax.fori_loop(0, x_ref.shape[0] // sub, sub_block, 0)


def _mix_in(x, mod, g_pre, params, tables, *, rows_per_mod, seq):
    rows, d = x.shape
    tm = min(ROW_TILE, rows_per_mod)
    per_mod = rows_per_mod // tm
    rope = tables is not None
    n_mod_blocks = rows // rows_per_mod
    if rope:
        grid = (seq // tm, n_mod_blocks)
        row_map = lambda t, b: (b * per_mod + t, 0)
        mod_map = lambda t, b: (b, 0, 0)
        tab_specs = [pl.BlockSpec((tm, LANES), lambda t, b: (t, 0)) for _ in tables]
    else:
        grid = (rows // tm, 1)
        row_map = lambda i, _: (i, 0)
        mod_map = lambda i, _: (i // per_mod, 0, 0)
        tab_specs = []
    out_widths = (256, 256, 256, 256, 256, 256, 256, 512, 512, 256)
    return pl.pallas_call(
        functools.partial(_mix_in_kernel, rope=rope),
        grid=grid,
        in_specs=[pl.BlockSpec((tm, d), row_map), pl.BlockSpec((1, N_MOD, d), mod_map),
                  _resident(g_pre.shape)] + [_resident(p.shape) for p in params] + tab_specs,
        out_specs=[pl.BlockSpec((tm, w), row_map) for w in out_widths],
        out_shape=[jax.ShapeDtypeStruct((rows, w), BF16) for w in out_widths],
        compiler_params=_compiler_params(2),
        name="mix_in_rope" if rope else "mix_in",
    )(x, mod, g_pre, *params, *(tables or ()))


def _attn_groups(kind):
    if kind == "a":
        return [(0, [(h * 64 + m * 32, h * 64 + (m + 1) * 32, h) for h in (2 * g, 2 * g + 1)
                     for m in (0, 1)]) for g in range(2)]
    if kind == "b":
        return [(0, [(h * 64, (h + 1) * 64, h) for h in (2 * g, 2 * g + 1)]) for g in range(2)]
    if kind == "d":
        return [(g, [(i * LANES, (i + 1) * LANES, 2 * g + i) for i in range(2)]) for g in range(2)]
    raise ValueError(kind)


ATTN_Q_TILE = dict(a=256, b=512, d=256)
HEAD_W = 64


def _first_chunk_keys(n_keys):
    half = n_keys // 2
    if n_keys >= 4 * MXU_DIM:
        half = -(-half // MXU_DIM) * MXU_DIM
    return half


def _attn_kernel(*refs, groups, n_src, tq, pair, lam_init):
    q_ref = refs[0]
    src = refs[1:1 + 2 * n_src]
    refs = refs[1 + 2 * n_src:]
    if pair:
        lam_ref, gsub_ref = refs[:2]
        refs = refs[2:]
    o_ref, k_scr, vt_scr, s_a, s_b = refs

    off = 0
    for i in range(n_src):
        n = src[2 * i].shape[0]
        k_scr[off:off + n, :] = src[2 * i][...]
        for r in range(0, n, MXU_DIM):
            blk = src[2 * i + 1][r:r + MXU_DIM, :].astype(F32)
            vt_scr[:, off + r:off + r + blk.shape[0]] = blk.T.astype(BF16)
        off += n
    n_keys = off
    half = _first_chunk_keys(n_keys)
    chunk0, chunk1 = slice(0, half), slice(half, n_keys)
    n_tiles = q_ref.shape[0] // tq
    lane_q = lax.broadcasted_iota(jnp.int32, (tq, MXU_DIM), 1)

    if pair:
        lv = lam_ref[...]
        lam = (jnp.exp(jnp.sum(lv[0:1] * lv[1:2], axis=-1, keepdims=True))
               - jnp.exp(jnp.sum(lv[2:3] * lv[3:4], axis=-1, keepdims=True)) + lam_init)

    def tile_rows(i):
        if isinstance(i, int):
            return slice(i * tq, (i + 1) * tq)
        return pl.ds(pl.multiple_of(i * tq, tq), tq)

    def scores_t(group, i, k_rows, s_dst):
        slab, maps = group
        qcols = slice(slab * MXU_DIM, (slab + 1) * MXU_DIM)
        qg = q_ref[tile_rows(i), qcols]
        zero = jnp.zeros_like(qg)
        qs = jnp.concatenate([jnp.where((lane_q >= lo) & (lane_q < hi), qg, zero)
                              for lo, hi, _ in maps], axis=0)
        s_dst[...] = _dot_nt(k_scr[k_rows, qcols], qs)

    def q_tile(group, i, prefetch):
        _, maps = group
        heads = sorted({h for _, _, h in maps})

        def weigh(s, m, k_cols):
            p = jnp.exp2(s - m)
            l = jnp.sum(p, axis=0, keepdims=True)
            pb = p.astype(BF16)
            acc = [_dot(vt_scr[h * HEAD_W:(h + 1) * HEAD_W, k_cols], pb[:, j * tq:(j + 1) * tq])
                   for j, (_, _, h) in enumerate(maps)]
            return l, acc

        scores_t(group, i, chunk1, s_b)
        s0 = s_a[...]
        m0 = jnp.max(s0, axis=0, keepdims=True)
        l0, acc0 = weigh(s0, m0, chunk0)
        if prefetch is not None:
            prefetch()
        s1 = s_b[...]
        m1 = jnp.maximum(m0, jnp.max(s1, axis=0, keepdims=True))
        alpha = jnp.exp2(m0 - m1)
        l1, acc1 = weigh(s1, m1, chunk1)
        inv_l = 1.0 / (l0 * alpha + l1)

        def out_t(j):
            cols = slice(j * tq, (j + 1) * tq)
            return (acc0[j] * alpha[:, cols] + acc1[j]) * inv_l[:, cols]

        rows = []
        for h in heads:
            js = [j for j, (_, _, hh) in enumerate(maps) if hh == h]
            if pair:
                o = out_t(js[0]) - lam * out_t(js[1])
                o = (o * lax.rsqrt(jnp.mean(o * o, axis=0, keepdims=True) + NORM_EPS)
                     * gsub_ref[...] * (1.0 - lam_init))
            else:
                o = out_t(js[0])
            rows.append(o)
        out = jnp.concatenate(rows, axis=0).T
        ocols = slice(heads[0] * HEAD_W, (heads[-1] + 1) * HEAD_W)
        o_ref[tile_rows(i), ocols] = out.astype(BF16)

    scores_t(groups[0], 0, chunk0, s_a)
    for gi, group in enumerate(groups):
        def body(i, carry, group=group):
            q_tile(group, i, lambda: scores_t(group, i + 1, chunk0, s_a))
            return carry

        if n_tiles > 1:
            lax.fori_loop(0, n_tiles - 1, body, 0)
        if gi + 1 < len(groups):
            q_tile(group, n_tiles - 1,
                   lambda nxt=groups[gi + 1]: scores_t(nxt, 0, chunk0, s_a))
        else:
            q_tile(group, n_tiles - 1, None)


def _attn(kind, q, sources, extra, *, n_batch, lam_init=0.0):
    groups = _attn_groups(kind)
    pair = kind == "a"
    rows, qw = q.shape
    sq = rows // n_batch
    tq = min(ATTN_Q_TILE[kind], sq)
    n_keys = sum(k.shape[0] // n_batch for k, _ in sources)
    out_w = sources[0][1].shape[1]
    cols = max(len(maps) for _, maps in groups) * tq
    in_specs = [pl.BlockSpec((sq, qw), lambda b: (b, 0))]
    args = [q]
    for k, v in sources:
        n = k.shape[0] // n_batch
        in_specs += [pl.BlockSpec((n, k.shape[1]), lambda b: (b, 0)),
                     pl.BlockSpec((n, v.shape[1]), lambda b: (b, 0))]
        args += [k, v]
    if pair:
        in_specs += [_resident(e.shape) for e in extra]
        args += list(extra)
    assert n_keys % (2 * LANES) == 0 and sq % tq == 0
    half = _first_chunk_keys(n_keys)
    return pl.pallas_call(
        functools.partial(_attn_kernel, groups=groups, n_src=len(sources), tq=tq, pair=pair,
                          lam_init=lam_init),
        grid=(n_batch,),
        in_specs=in_specs,
        out_specs=pl.BlockSpec((sq, out_w), lambda b: (b, 0)),
        out_shape=jax.ShapeDtypeStruct((rows, out_w), BF16),
        scratch_shapes=[pltpu.VMEM((n_keys, qw), BF16), pltpu.VMEM((out_w, n_keys), BF16),
                        pltpu.VMEM((half, cols), F32), pltpu.VMEM((n_keys - half, cols), F32)],
        compiler_params=_compiler_params(1),
        name=f"attn_{kind}_{len(sources)}",
    )(*args)


def _rope_table(seq, pattern):
    n_rows = seq // GRID_W
    n_freq = next(e[2] for e in pattern if e is not None)
    axis_dim = 2 * n_freq
    inv_freq = ROPE_THETA ** (-jnp.arange(0, axis_dim, 2, dtype=F32) / axis_dim)
    active = np.array([e is not None for e in pattern])
    axis = np.array([e[0] if e is not None else 0 for e in pattern])
    freq = np.array([e[1] if e is not None else 0 for e in pattern])
    first = np.array([bool(e[3]) if e is not None else False for e in pattern])

    def per_position(n, fn):
        return fn(jnp.arange(n, dtype=F32)[:, None] * inv_freq[freq][None, :])

    def full(fn):
        by_row = jnp.repeat(per_position(n_rows, fn), GRID_W, axis=0)
        by_col = jnp.tile(per_position(GRID_W, fn), (n_rows, 1))
        return jnp.where(axis[None, :] == 0, by_row, by_col)

    cos = jnp.where(active[None, :], full(jnp.cos), 1.0)
    sin = jnp.where(active[None, :], full(jnp.sin), 0.0)
    return cos, jnp.where(first[None, :], -sin, sin)


def _rope_pattern(rot_dim):
    n_freq = rot_dim // 4
    return [(r // (2 * n_freq), r % n_freq, n_freq, (r % (2 * n_freq)) < n_freq)
            for r in range(rot_dim)]


def _rope_tables(seq):
    p32 = _rope_pattern(32) * (LANES // 32)
    p64 = _rope_pattern(64) * (LANES // 64)
    pd = [None] * 64 + _rope_pattern(32) + [None] * 32
    return _rope_table(seq, p32) + _rope_table(seq, p64) + _rope_table(seq, pd)


def _layer_params(l, w_in, g_qnorm, g_knorm, w_spatial, b_spatial, ln_g, ln_b, g_q_a, w_uq,
                  g_kv_a, w_ukv):
    d_model = w_in.shape[1]
    sizes = (256, 256, 256, 256, 128, 128, 256, 256, 256, 128, 32)
    offs = [0]
    for s in sizes:
        offs.append(offs[-1] + s)
    aq, ak, av, bq, bk, bv, cu, cv, dq, dkv, dkr = (w_in[l][:, offs[i]:offs[i + 1]]
                                                    for i in range(11))
    log2e = math.log2(math.e)
    a_scale = 32 ** -0.5 * log2e
    b_scale = 64 ** -0.5 * log2e
    d_scale = 96 ** -0.5 * log2e

    def rep_kv(w):
        hd = w.shape[1] // B_KV_HEADS
        n_rep = B_HEADS // B_KV_HEADS
        return jnp.concatenate([w[:, (h // n_rep) * hd:(h // n_rep + 1) * hd]
                                for h in range(B_HEADS)], axis=1)

    zeros = lambda n: jnp.zeros((d_model, n), w_in.dtype)
    dkr_placed = jnp.concatenate([zeros(64), dkr, zeros(32)], axis=1)
    blocks = dict(aq=aq * a_scale, ak=ak, av=av, bq=bq, bk=rep_kv(bk), bv=rep_kv(bv), cu=cu, cv=cv,
                  dq=dq, dkr=dkr_placed, dkv=dkv)
    w_ext = jnp.concatenate([blocks[n] for n in _IN_BLOCKS], axis=1).astype(BF16)

    def tile_row(v, n, scale=1.0):
        return (jnp.tile(v, n) * scale).reshape(1, -1)

    bd64 = jnp.kron(jnp.eye(4, dtype=F32), jnp.full((64, 64), 1.0 / 64, F32)).astype(BF16)
    wsp = w_spatial[l].reshape(C_GROUPS * CHUNK, CHUNK).astype(BF16)
    bsp = jnp.repeat(b_spatial[l].T, 256 // C_GROUPS, axis=1)

    uq = w_uq[l].reshape(-1, D_HEADS, 96) * d_scale
    uq = jnp.concatenate([uq, jnp.zeros(uq.shape[:2] + (32,), uq.dtype)], axis=-1)
    wuq = uq.reshape(uq.shape[0], D_HEADS * LANES).astype(BF16)
    ukv = w_ukv[l].reshape(-1, D_HEADS, 128)
    ukv_k = jnp.concatenate([ukv[..., :64], jnp.zeros(ukv.shape[:2] + (64,), ukv.dtype)], axis=-1)
    wukvk = ukv_k.reshape(ukv.shape[0], D_HEADS * LANES).astype(BF16)
    wukvv = ukv[..., 64:].reshape(ukv.shape[0], D_HEADS * 64).astype(BF16)

    return (w_ext, tile_row(g_qnorm[l], B_HEADS, b_scale), tile_row(g_knorm[l], B_HEADS), bd64,
            ln_g[l].reshape(1, -1), ln_b[l].reshape(1, -1), wsp, bsp,
            g_q_a[l].reshape(1, -1), wuq, g_kv_a[l].reshape(1, -1), wukvk, wukvv), bd64


def _ffn_weights(w_in, w_out):
    d, two_ff = w_in.shape
    d_ff = two_ff // 2
    n = d_ff // FF_TILE
    head = d_ff - n * FF_TILE
    if head == 0:
        n, head = n - 1, FF_TILE
    wg, wu = w_in[:, :d_ff].astype(BF16), w_in[:, d_ff:].astype(BF16)
    wd = w_out.astype(BF16)

    def main(w):
        return w[:, head:].reshape(d, n, FF_TILE).transpose(1, 0, 2)

    return (wg[:, :head], wu[:, :head], wd[:head], main(wg), main(wu),
            wd[head:].reshape(n, FF_TILE, d))


def _lambda_init(layer_idx):
    return 0.8 - 0.6 * math.exp(-0.3 * layer_idx)


def kernel(x, c, ctx, c_ctx, w_ada, b_ada, g_pre, g_post, w_ffn1_in, w_ffn1_out, w_ffn2_in,
           w_ffn2_out, w_in, w_out, lam_vecs, g_subln, g_qnorm, g_knorm, w_spatial, b_spatial,
           ln_g, ln_b, g_q_a, w_uq, g_kv_a, w_ukv):
    n_batch, seq, d = x.shape
    n_ctx = ctx.shape[1]
    depth = w_ada.shape[0]

    n_cond = n_batch + 1
    pad = (-n_cond) % 8
    cc = jnp.concatenate([c, c_ctx[None, :], jnp.zeros((pad, d), c.dtype)], axis=0)
    mod = _ada(cc, w_ada, b_ada)
    mod_lat = mod[:, :n_batch].reshape(depth, n_batch, N_MOD, d)
    mod_ctx = mod[:, n_batch:n_cond].reshape(depth, 1, N_MOD, d)

    tables = _rope_tables(seq)
    x_lat = x.reshape(n_batch * seq, d)
    x_ctx = ctx.reshape(n_batch * n_ctx, d)
    lat = dict(rows_per_mod=seq)
    ctxk = dict(rows_per_mod=n_batch * n_ctx)

    for l in range(depth):
        need_ctx = l < depth - 1
        ffn1 = _ffn_weights(w_ffn1_in[l], w_ffn1_out[l])
        ffn2 = _ffn_weights(w_ffn2_in[l], w_ffn2_out[l])
        params, bd64 = _layer_params(l, w_in, g_qnorm, g_knorm, w_spatial, b_spatial, ln_g, ln_b,
                                     g_q_a, w_uq, g_kv_a, w_ukv)
        wo = w_out[l].reshape(N_MIXERS, -1, d).astype(BF16)
        a_extra = (lam_vecs[l], g_subln[l].reshape(-1, 1))
        lam0 = _lambda_init(l)

        x_ctx = _ffn(x_ctx, mod_ctx[l], g_pre[l], g_post[l], ffn1, j=0, **ctxk)
        x_lat = _ffn(x_lat, mod_lat[l], g_pre[l], g_post[l], ffn1, j=0, **lat)

        pc = _mix_in(x_ctx, mod_ctx[l], g_pre[l], params, None, seq=n_ctx, **ctxk)
        pl_ = _mix_in(x_lat, mod_lat[l], g_pre[l], params, tables, seq=seq, **lat)
        qa_c, ka_c, va_c, qb_c, kb_c, vb_c, cc_c, qd_c, kd_c, vd_c = pc
        qa_l, ka_l, va_l, qb_l, kb_l, vb_l, cc_l, qd_l, kd_l, vd_l = pl_

        a_l = _attn("a", qa_l, [(ka_l, va_l), (ka_c, va_c)], a_extra, n_batch=n_batch, lam_init=lam0)
        b_l = _attn("b", qb_l, [(kb_l, vb_l), (kb_c, vb_c)], (), n_batch=n_batch)
        d_l = _attn("d", qd_l, [(kd_l, vd_l), (kd_c, vd_c)], (), n_batch=n_batch)
        x_lat = _ffn(x_lat, mod_lat[l], g_pre[l], g_post[l], ffn2, j=2,
                     mixer=((a_l, b_l, cc_l, d_l), wo), **lat)

        if need_ctx:
            a_c = _attn("a", qa_c, [(ka_c, va_c)], a_extra, n_batch=n_batch, lam_init=lam0)
            b_c = _attn("b", qb_c, [(kb_c, vb_c)], (), n_batch=n_batch)
            d_c = _attn("d", qd_c, [(kd_c, vd_c)], (), n_batch=n_batch)
            x_ctx = _ffn(x_ctx, mod_ctx[l], g_pre[l], g_post[l], ffn2, j=2,
                         mixer=((a_c, b_c, cc_c, d_c), wo), **ctxk)

    return x_lat.reshape(n_batch, seq, d)
```

```python
import functools
import math

import jax
import jax.numpy as jnp
import numpy as np
from jax import lax
from jax.experimental import pallas as pl
from jax.experimental.pallas import tpu as pltpu

F32 = jnp.float32
BF16 = jnp.bfloat16

GRID_W = 64
CHUNK = 128
ROPE_THETA = 10000.0
NORM_EPS = 1e-6
FFN_RES = 0.5
N_MOD = 9
N_MIXERS = 4
A_HEADS, B_HEADS, B_KV_HEADS, C_GROUPS, D_HEADS = 4, 4, 2, 4, 4

LANES = 128
MXU_DIM = 256
VMEM_LIMIT_BYTES = 58 * 2**20

ROW_TILE = 1024
FF_TILE = 1024
MIX_SUB_ROWS = 1024


def _dot(a, b):
    return jnp.dot(a, b, preferred_element_type=F32)


def _dot_nt(a, b):
    return lax.dot_general(a, b, (((1,), (1,)), ((), ())), preferred_element_type=F32)


def _split_bf16(x):
    hi = x.astype(BF16)
    lo = (x - hi.astype(F32)).astype(BF16)
    return hi, lo


def _group_mean(xsq, bd_ref):
    hi, lo = _split_bf16(xsq)
    bd = bd_ref[...]
    return _dot(hi, bd) + _dot(lo, bd)


def _rms(x):
    return x * lax.rsqrt(jnp.mean(x * x, axis=-1, keepdims=True) + NORM_EPS)


def _pre_norm(x, mod_ref, gpre_ref, j):
    shift = mod_ref[0, 3 * j:3 * j + 1, :]
    gain = gpre_ref[j:j + 1, :] * (1.0 + mod_ref[0, 3 * j + 1:3 * j + 2, :])
    return _rms(x) * gain + shift


def _post_norm(x, y, mod_ref, gpost_ref, j, res):
    gain = (res * mod_ref[0, 3 * j + 2:3 * j + 3, :]) * gpost_ref[j:j + 1, :]
    return x + _rms(y) * gain


def _gelu_tanh(x):
    c = math.sqrt(2.0 / math.pi)
    return x * (0.5 * (1.0 + jnp.tanh(c * (x + 0.044715 * (x * x * x)))))


def _compiler_params(n_axes):
    return pltpu.CompilerParams(dimension_semantics=("arbitrary",) * n_axes,
                                vmem_limit_bytes=VMEM_LIMIT_BYTES)


def _resident(shape):
    nd = len(shape)
    return pl.BlockSpec(shape, lambda *_: (0,) * nd, pipeline_mode=pl.Buffered(1))


def _ada_kernel(c_ref, w_ref, b_ref, o_ref):
    c = c_ref[...]
    s = c * jax.nn.sigmoid(c)
    s_hi, s_lo = _split_bf16(s)
    w_hi, w_lo = _split_bf16(w_ref[0])
    o_ref[0] = _dot(s_hi, w_hi) + (_dot(s_hi, w_lo) + _dot(s_lo, w_hi)) + b_ref[0]


def _ada(cc, w_ada, b_ada):
    n_layers, d, n = w_ada.shape
    rows = cc.shape[0]
    tn = 1024
    return pl.pallas_call(
        _ada_kernel,
        grid=(n_layers, n // tn),
        in_specs=[pl.BlockSpec((rows, d), lambda l, j: (0, 0)),
                  pl.BlockSpec((1, d, tn), lambda l, j: (l, 0, j)),
                  pl.BlockSpec((1, 1, tn), lambda l, j: (l, 0, j))],
        out_specs=pl.BlockSpec((1, rows, tn), lambda l, j: (l, 0, j)),
        out_shape=jax.ShapeDtypeStruct((n_layers, rows, n), F32),
        compiler_params=_compiler_params(2),
        name="ada",
    )(cc, w_ada, b_ada.reshape(n_layers, 1, n))


def _ffn_kernel(*refs, j, n_mix):
    x_ref, mod_ref, gpre_ref, gpost_ref, w1_ref, w2_ref = refs[:6]
    mix_refs = refs[6:6 + n_mix]
    o_ref, xn_scr, acc_scr = refs[6 + n_mix:]
    d_ff = w2_ref.shape[0]

    def swiglu_chunk(xn, wg, wu, wd):
        hg = _dot(xn, wg)
        hu = _dot(xn, wu)
        return _dot(((hg * jax.nn.sigmoid(hg)) * hu).astype(BF16), wd)

    x = x_ref[...]
    if n_mix:
        wout_ref = mix_refs[-1]
        y = _dot(mix_refs[0][...], wout_ref[0])
        for g in range(1, n_mix - 1):
            y += _dot(mix_refs[g][...], wout_ref[g])
        x = _post_norm(x, y, mod_ref, gpost_ref, 1, 1.0)

    xn_scr[...] = _pre_norm(x, mod_ref, gpre_ref, j).astype(BF16)
    first = d_ff % FF_TILE or FF_TILE
    bounds = [0] + list(range(first, d_ff + 1, FF_TILE))
    for ci, (a, b) in enumerate(zip(bounds[:-1], bounds[1:])):
        t = swiglu_chunk(xn_scr[...], w1_ref[:, a:b], w1_ref[:, d_ff + a:d_ff + b], w2_ref[a:b, :])
        acc_scr[...] = t if ci == 0 else acc_scr[...] + t
    o_ref[...] = _post_norm(x, acc_scr[...], mod_ref, gpost_ref, j, FFN_RES)


def _ffn(x, mod, g_pre, g_post, weights, *, j, rows_per_mod, mixer=None):
    rows, d = x.shape
    tm = min(ROW_TILE, rows_per_mod)
    per_mod = rows_per_mod // tm
    mix_args, mix_specs = [], []
    if mixer is not None:
        groups, w_out = mixer
        mix_args = list(groups) + [w_out]
        mix_specs = ([pl.BlockSpec((tm, g.shape[1]), lambda i: (i, 0)) for g in groups]
                     + [_resident(w_out.shape)])
    return pl.pallas_call(
        functools.partial(_ffn_kernel, j=j, n_mix=len(mix_args)),
        grid=(rows // tm,),
        in_specs=[pl.BlockSpec((tm, d), lambda i: (i, 0)),
                  pl.BlockSpec((1, N_MOD, d), lambda i: (i // per_mod, 0, 0)),
                  _resident(g_pre.shape), _resident(g_post.shape)]
                 + [_resident(w.shape) for w in weights] + mix_specs,
        out_specs=pl.BlockSpec((tm, d), lambda i: (i, 0)),
        out_shape=jax.ShapeDtypeStruct((rows, d), F32),
        scratch_shapes=[pltpu.VMEM((tm, d), BF16), pltpu.VMEM((tm, d), F32)],
        compiler_params=_compiler_params(1),
        name=f"ffn{j}" if mixer is None else f"mix_out_ffn{j}",
    )(x, mod, g_pre, g_post, *weights, *mix_args)


_IN_BLOCKS = ("aq", "ak", "av", "bq", "bk", "bv", "cu", "cv", "dq", "dkr", "dkv")
_IN_WIDTHS = dict(aq=256, ak=256, av=256, bq=256, bk=256, bv=256, cu=256, cv=256,
                  dq=256, dkr=128, dkv=128)
_IN_OFFSETS = {}
_off = 0
for _n in _IN_BLOCKS:
    _IN_OFFSETS[_n] = _off
    _off += _IN_WIDTHS[_n]
IN_EXT_WIDTH = _off


def _rope_slabs(x, tabs, shift, store):
    cos, sin = tabs
    if cos is not None:
        lane = lax.broadcasted_iota(jnp.int32, cos.shape, 1)
        first = (lane & (2 * shift - 1)) < shift
    for s in range(x.shape[1] // LANES):
        xs = x[:, s * LANES:(s + 1) * LANES]
        if cos is not None:
            partner = jnp.where(first, pltpu.roll(xs, LANES - shift, axis=1),
                                pltpu.roll(xs, shift, axis=1))
            xs = xs * cos + partner * sin
        store(s, xs.astype(BF16))


def _mix_in_kernel(*refs, rope):
    (x_ref, mod_ref, gpre_ref, win_ref, gq_ref, gk_ref, bd64_ref, lng_ref, lnb_ref, wsp_ref,
     bsp_ref, gqa_ref, wuq_ref, gkva_ref, wukvk_ref, wukvv_ref) = refs[:16]
    refs = refs[16:]
    tab_refs = refs[:6] if rope else ()
    refs = refs[6:] if rope else refs
    qa_ref, ka_ref, va_ref, qb_ref, kb_ref, vb_ref, cc_ref, qd_ref, kd_ref, vd_ref = refs
    sub = min(MIX_SUB_ROWS, x_ref.shape[0])

    def sub_block(r, carry):
        rows = pl.ds(pl.multiple_of(r * sub, sub), sub)
        if rope:
            tabs = [t[rows, :] for t in tab_refs]
            t32, t64, td = tabs[0:2], tabs[2:4], tabs[4:6]
        else:
            t32 = t64 = td = (None, None)

        xn = _pre_norm(x_ref[rows, :], mod_ref, gpre_ref, 1).astype(BF16)

        def proj(name):
            o = _IN_OFFSETS[name]
            return _dot(xn, win_ref[:, o:o + _IN_WIDTHS[name]])

        def slab_store(ref):
            def store(s, v):
                ref[rows, s * LANES:(s + 1) * LANES] = v
            return store

        _rope_slabs(proj("aq"), t32, 8, slab_store(qa_ref))
        _rope_slabs(proj("ak"), t32, 8, slab_store(ka_ref))
        va_ref[rows, :] = proj("av").astype(BF16)

        def qk_norm(p, g_ref):
            return p * lax.rsqrt(_group_mean(p * p, bd64_ref) + NORM_EPS) * g_ref[...]

        _rope_slabs(qk_norm(proj("bq"), gq_ref), t64, 16, slab_store(qb_ref))
        _rope_slabs(qk_norm(proj("bk"), gk_ref), t64, 16, slab_store(kb_ref))
        vb_ref[rows, :] = proj("bv").astype(BF16)

        u = _gelu_tanh(proj("cu"))
        v = _gelu_tanh(proj("cv"))
        mu = jnp.mean(v, axis=-1, keepdims=True)
        vc = v - mu
        vln = (vc * lax.rsqrt(jnp.mean(vc * vc, axis=-1, keepdims=True) + NORM_EPS) * lng_ref[...]
               + lnb_ref[...]).astype(BF16)
        width = vln.shape[1]
        gw = width // C_GROUPS
        lane = lax.broadcasted_iota(jnp.int32, (CHUNK, width), 1)
        wsp = wsp_ref[...]
        bsp = bsp_ref[...]
        gated = []
        for ci in range(sub // CHUNK):
            crows = slice(ci * CHUNK, (ci + 1) * CHUNK)
            allg = _dot(wsp, vln[crows, :])
            mixed = allg[0:CHUNK]
            for g in range(1, C_GROUPS):
                mixed = jnp.where(lane >= g * gw, allg[g * CHUNK:(g + 1) * CHUNK], mixed)
            gated.append((u[crows, :] * (mixed + bsp)).astype(BF16))
        cc_ref[rows, :] = jnp.concatenate(gated, axis=0)

        cq = (_rms(proj("dq")) * gqa_ref[...]).astype(BF16)
        _rope_slabs(_dot(cq, wuq_ref[...]), td, 8, slab_store(qd_ref))
        ckv = (_rms(proj("dkv")) * gkva_ref[...]).astype(BF16)
        kr = proj("dkr")
        _rope_slabs(_dot(ckv, wukvk_ref[...]) + jnp.concatenate([kr] * D_HEADS, axis=1), td, 8,
                    slab_store(kd_ref))
        vd_ref[rows, :] = _dot(ckv, wukvv_ref[...]).astype(BF16)
        return carry

    lax.fori_loop(0, x_ref.shape[0] // sub, sub_block, 0)


def _mix_in(x, mod, g_pre, params, tables, *, rows_per_mod, seq):
    rows, d = x.shape
    tm = min(ROW_TILE, rows_per_mod)
    per_mod = rows_per_mod // tm
    rope = tables is not None
    n_mod_blocks = rows // rows_per_mod
    if rope:
        grid = (seq // tm, n_mod_blocks)
        row_map = lambda t, b: (b * per_mod + t, 0)
        mod_map = lambda t, b: (b, 0, 0)
        tab_specs = [pl.BlockSpec((tm, LANES), lambda t, b: (t, 0)) for _ in tables]
    else:
        grid = (rows // tm, 1)
        row_map = lambda i, _: (i, 0)
        mod_map = lambda i, _: (i // per_mod, 0, 0)
        tab_specs = []
    out_widths = (256, 256, 256, 256, 256, 256, 256, 512, 512, 256)
    return pl.pallas_call(
        functools.partial(_mix_in_kernel, rope=rope),
        grid=grid,
        in_specs=[pl.BlockSpec((tm, d), row_map), pl.BlockSpec((1, N_MOD, d), mod_map),
                  _resident(g_pre.shape)] + [_resident(p.shape) for p in params] + tab_specs,
        out_specs=[pl.BlockSpec((tm, w), row_map) for w in out_widths],
        out_shape=[jax.ShapeDtypeStruct((rows, w), BF16) for w in out_widths],
        compiler_params=_compiler_params(2),
        name="mix_in_rope" if rope else "mix_in",
    )(x, mod, g_pre, *params, *(tables or ()))


def _attn_groups(kind):
    if kind == "a":
        return [(0, [(h * 64 + m * 32, h * 64 + (m + 1) * 32, h) for h in (2 * g, 2 * g + 1)
                     for m in (0, 1)]) for g in range(2)]
    if kind == "b":
        return [(0, [(h * 64, (h + 1) * 64, h) for h in (2 * g, 2 * g + 1)]) for g in range(2)]
    if kind == "d":
        return [(g, [(i * LANES, (i + 1) * LANES, 2 * g + i) for i in range(2)]) for g in range(2)]
    raise ValueError(kind)


ATTN_Q_TILE = dict(a=256, b=512, d=256)
HEAD_W = 64
ATTN_KEY_CHUNKS = 2


def _key_chunks(n_keys):
    tiles, ragged = divmod(n_keys, MXU_DIM)
    if ragged or tiles < ATTN_KEY_CHUNKS:
        half = n_keys // 2
        return [(0, half), (half, n_keys)]
    base, extra = divmod(tiles, ATTN_KEY_CHUNKS)
    bounds = [0]
    for c in range(ATTN_KEY_CHUNKS):
        bounds.append(bounds[-1] + (base + (c < extra)) * MXU_DIM)
    return list(zip(bounds[:-1], bounds[1:]))


def _attn_kernel(*refs, groups, n_src, tq, pair, lam_init):
    q_ref = refs[0]
    src = refs[1:1 + 2 * n_src]
    refs = refs[1 + 2 * n_src:]
    if pair:
        lam_ref, gsub_ref = refs[:2]
        refs = refs[2:]
    o_ref, k_scr, vt_scr, s_a, s_b = refs

    off = 0
    for i in range(n_src):
        n = src[2 * i].shape[0]
        k_scr[off:off + n, :] = src[2 * i][...]
        for r in range(0, n, MXU_DIM):
            blk = src[2 * i + 1][r:r + MXU_DIM, :].astype(F32)
            vt_scr[:, off + r:off + r + blk.shape[0]] = blk.T.astype(BF16)
        off += n
    n_keys = off
    chunks = [slice(a, b) for a, b in _key_chunks(n_keys)]
    bufs = (s_a, s_b)
    n_tiles = q_ref.shape[0] // tq
    lane_q = lax.broadcasted_iota(jnp.int32, (tq, MXU_DIM), 1)

    if pair:
        lv = lam_ref[...]
        lam = (jnp.exp(jnp.sum(lv[0:1] * lv[1:2], axis=-1, keepdims=True))
               - jnp.exp(jnp.sum(lv[2:3] * lv[3:4], axis=-1, keepdims=True)) + lam_init)

    def tile_rows(i):
        if isinstance(i, int):
            return slice(i * tq, (i + 1) * tq)
        return pl.ds(pl.multiple_of(i * tq, tq), tq)

    def scores_t(group, i, k_rows, s_dst):
        slab, maps = group
        qcols = slice(slab * MXU_DIM, (slab + 1) * MXU_DIM)
        qg = q_ref[tile_rows(i), qcols]
        zero = jnp.zeros_like(qg)
        qs = jnp.concatenate([jnp.where((lane_q >= lo) & (lane_q < hi), qg, zero)
                              for lo, hi, _ in maps], axis=0)
        s_dst[...] = _dot_nt(k_scr[k_rows, qcols], qs)

    def q_tile(group, i, prefetch):
        _, maps = group
        heads = sorted({h for _, _, h in maps})

        def weigh(s, m, k_cols):
            p = jnp.exp2(s - m)
            l = jnp.sum(p, axis=0, keepdims=True)
            pb = p.astype(BF16)
            acc = [_dot(vt_scr[h * HEAD_W:(h + 1) * HEAD_W, k_cols], pb[:, j * tq:(j + 1) * tq])
                   for j, (_, _, h) in enumerate(maps)]
            return l, acc

        m = l = acc = None
        for c, k_cols in enumerate(chunks):
            if c + 1 < len(chunks):
                nxt = chunks[c + 1]
                scores_t(group, i, nxt, bufs[(c + 1) % 2].at[:nxt.stop - nxt.start])
            elif prefetch is not None:
                prefetch()
            s = bufs[c % 2][:k_cols.stop - k_cols.start, :]
            cm = jnp.max(s, axis=0, keepdims=True)
            if m is None:
                m = cm
                l, acc = weigh(s, m, k_cols)
            else:
                m_new = jnp.maximum(m, cm)
                alpha = jnp.exp2(m - m_new)
                l_c, acc_c = weigh(s, m_new, k_cols)
                l = l * alpha + l_c
                acc = [a * alpha[:, j * tq:(j + 1) * tq] + a_c
                       for j, (a, a_c) in enumerate(zip(acc, acc_c))]
                m = m_new
        inv_l = 1.0 / l

        def out_t(j):
            return acc[j] * inv_l[:, j * tq:(j + 1) * tq]

        rows = []
        for h in heads:
            js = [j for j, (_, _, hh) in enumerate(maps) if hh == h]
            if pair:
                o = out_t(js[0]) - lam * out_t(js[1])
                o = (o * lax.rsqrt(jnp.mean(o * o, axis=0, keepdims=True) + NORM_EPS)
                     * gsub_ref[...] * (1.0 - lam_init))
            else:
                o = out_t(js[0])
            rows.append(o)
        out = jnp.concatenate(rows, axis=0).T
        ocols = slice(heads[0] * HEAD_W, (heads[-1] + 1) * HEAD_W)
        o_ref[tile_rows(i), ocols] = out.astype(BF16)

    first = s_a.at[:chunks[0].stop]
    scores_t(groups[0], 0, chunks[0], first)
    for gi, group in enumerate(groups):
        def body(i, carry, group=group):
            q_tile(group, i, lambda: scores_t(group, i + 1, chunks[0], first))
            return carry

        if n_tiles > 1:
            lax.fori_loop(0, n_tiles - 1, body, 0)
        if gi + 1 < len(groups):
            q_tile(group, n_tiles - 1,
                   lambda nxt=groups[gi + 1]: scores_t(nxt, 0, chunks[0], first))
        else:
            q_tile(group, n_tiles - 1, None)


def _attn(kind, q, sources, extra, *, n_batch, lam_init=0.0):
    groups = _attn_groups(kind)
    pair = kind == "a"
    rows, qw = q.shape
    sq = rows // n_batch
    tq = min(ATTN_Q_TILE[kind], sq)
    n_keys = sum(k.shape[0] // n_batch for k, _ in sources)
    out_w = sources[0][1].shape[1]
    cols = max(len(maps) for _, maps in groups) * tq
    in_specs = [pl.BlockSpec((sq, qw), lambda b: (b, 0))]
    args = [q]
    for k, v in sources:
        n = k.shape[0] // n_batch
        in_specs += [pl.BlockSpec((n, k.shape[1]), lambda b: (b, 0)),
                     pl.BlockSpec((n, v.shape[1]), lambda b: (b, 0))]
        args += [k, v]
    if pair:
        in_specs += [_resident(e.shape) for e in extra]
        args += list(extra)
    assert n_keys % (2 * LANES) == 0 and sq % tq == 0
    sizes = [b - a for a, b in _key_chunks(n_keys)]
    return pl.pallas_call(
        functools.partial(_attn_kernel, groups=groups, n_src=len(sources), tq=tq, pair=pair,
                          lam_init=lam_init),
        grid=(n_batch,),
        in_specs=in_specs,
        out_specs=pl.BlockSpec((sq, out_w), lambda b: (b, 0)),
        out_shape=jax.ShapeDtypeStruct((rows, out_w), BF16),
        scratch_shapes=[pltpu.VMEM((n_keys, qw), BF16), pltpu.VMEM((out_w, n_keys), BF16),
                        pltpu.VMEM((max(sizes[0::2]), cols), F32),
                        pltpu.VMEM((max(sizes[1::2]), cols), F32)],
        compiler_params=_compiler_params(1),
        name=f"attn_{kind}_{len(sources)}",
    )(*args)


def _rope_table(seq, pattern):
    n_rows = seq // GRID_W
    n_freq = next(e[2] for e in pattern if e is not None)
    axis_dim = 2 * n_freq
    inv_freq = ROPE_THETA ** (-jnp.arange(0, axis_dim, 2, dtype=F32) / axis_dim)
    active = np.array([e is not None for e in pattern])
    axis = np.array([e[0] if e is not None else 0 for e in pattern])
    freq = np.array([e[1] if e is not None else 0 for e in pattern])
    first = np.array([bool(e[3]) if e is not None else False for e in pattern])

    def per_position(n, fn):
        return fn(jnp.arange(n, dtype=F32)[:, None] * inv_freq[freq][None, :])

    def full(fn):
        by_row = jnp.repeat(per_position(n_rows, fn), GRID_W, axis=0)
        by_col = jnp.tile(per_position(GRID_W, fn), (n_rows, 1))
        return jnp.where(axis[None, :] == 0, by_row, by_col)

    cos = jnp.where(active[None, :], full(jnp.cos), 1.0)
    sin = jnp.where(active[None, :], full(jnp.sin), 0.0)
    return cos, jnp.where(first[None, :], -sin, sin)


def _rope_pattern(rot_dim):
    n_freq = rot_dim // 4
    return [(r // (2 * n_freq), r % n_freq, n_freq, (r % (2 * n_freq)) < n_freq)
            for r in range(rot_dim)]


def _rope_tables(seq):
    p32 = _rope_pattern(32) * (LANES // 32)
    p64 = _rope_pattern(64) * (LANES // 64)
    pd = [None] * 64 + _rope_pattern(32) + [None] * 32
    return _rope_table(seq, p32) + _rope_table(seq, p64) + _rope_table(seq, pd)


def _layer_params(l, w_in, g_qnorm, g_knorm, w_spatial, b_spatial, ln_g, ln_b, g_q_a, w_uq,
                  g_kv_a, w_ukv):
    d_model = w_in.shape[1]
    sizes = (256, 256, 256, 256, 128, 128, 256, 256, 256, 128, 32)
    offs = [0]
    for s in sizes:
        offs.append(offs[-1] + s)
    aq, ak, av, bq, bk, bv, cu, cv, dq, dkv, dkr = (w_in[l][:, offs[i]:offs[i + 1]]
                                                    for i in range(11))
    log2e = math.log2(math.e)
    a_scale = 32 ** -0.5 * log2e
    b_scale = 64 ** -0.5 * log2e
    d_scale = 96 ** -0.5 * log2e

    def rep_kv(w):
        hd = w.shape[1] // B_KV_HEADS
        n_rep = B_HEADS // B_KV_HEADS
        return jnp.concatenate([w[:, (h // n_rep) * hd:(h // n_rep + 1) * hd]
                                for h in range(B_HEADS)], axis=1)

    zeros = lambda n: jnp.zeros((d_model, n), w_in.dtype)
    dkr_placed = jnp.concatenate([zeros(64), dkr, zeros(32)], axis=1)
    blocks = dict(aq=aq * a_scale, ak=ak, av=av, bq=bq, bk=rep_kv(bk), bv=rep_kv(bv), cu=cu, cv=cv,
                  dq=dq, dkr=dkr_placed, dkv=dkv)
    w_ext = jnp.concatenate([blocks[n] for n in _IN_BLOCKS], axis=1).astype(BF16)

    def tile_row(v, n, scale=1.0):
        return (jnp.tile(v, n) * scale).reshape(1, -1)

    bd64 = jnp.kron(jnp.eye(4, dtype=F32), jnp.full((64, 64), 1.0 / 64, F32)).astype(BF16)
    wsp = w_spatial[l].reshape(C_GROUPS * CHUNK, CHUNK).astype(BF16)
    bsp = jnp.repeat(b_spatial[l].T, 256 // C_GROUPS, axis=1)

    uq = w_uq[l].reshape(-1, D_HEADS, 96) * d_scale
    uq = jnp.concatenate([uq, jnp.zeros(uq.shape[:2] + (32,), uq.dtype)], axis=-1)
    wuq = uq.reshape(uq.shape[0], D_HEADS * LANES).astype(BF16)
    ukv = w_ukv[l].reshape(-1, D_HEADS, 128)
    ukv_k = jnp.concatenate([ukv[..., :64], jnp.zeros(ukv.shape[:2] + (64,), ukv.dtype)], axis=-1)
    wukvk = ukv_k.reshape(ukv.shape[0], D_HEADS * LANES).astype(BF16)
    wukvv = ukv[..., 64:].reshape(ukv.shape[0], D_HEADS * 64).astype(BF16)

    return (w_ext, tile_row(g_qnorm[l], B_HEADS, b_scale), tile_row(g_knorm[l], B_HEADS), bd64,
            ln_g[l].reshape(1, -1), ln_b[l].reshape(1, -1), wsp, bsp,
            g_q_a[l].reshape(1, -1), wuq, g_kv_a[l].reshape(1, -1), wukvk, wukvv), bd64


def _ffn_weights(w_in, w_out):
    return w_in.astype(BF16), w_out.astype(BF16)


def _lambda_init(layer_idx):
    return 0.8 - 0.6 * math.exp(-0.3 * layer_idx)


def kernel(x, c, ctx, c_ctx, w_ada, b_ada, g_pre, g_post, w_ffn1_in, w_ffn1_out, w_ffn2_in,
           w_ffn2_out, w_in, w_out, lam_vecs, g_subln, g_qnorm, g_knorm, w_spatial, b_spatial,
           ln_g, ln_b, g_q_a, w_uq, g_kv_a, w_ukv):
    n_batch, seq, d = x.shape
    n_ctx = ctx.shape[1]
    depth = w_ada.shape[0]

    n_cond = n_batch + 1
    pad = (-n_cond) % 8
    cc = jnp.concatenate([c, c_ctx[None, :], jnp.zeros((pad, d), c.dtype)], axis=0)
    mod = _ada(cc, w_ada, b_ada)
    mod_lat = mod[:, :n_batch].reshape(depth, n_batch, N_MOD, d)
    mod_ctx = mod[:, n_batch:n_cond].reshape(depth, 1, N_MOD, d)

    tables = _rope_tables(seq)
    x_lat = x.reshape(n_batch * seq, d)
    x_ctx = ctx.reshape(n_batch * n_ctx, d)
    lat = dict(rows_per_mod=seq)
    ctxk = dict(rows_per_mod=n_batch * n_ctx)

    for l in range(depth):
        need_ctx = l < depth - 1
        ffn1 = _ffn_weights(w_ffn1_in[l], w_ffn1_out[l])
        ffn2 = _ffn_weights(w_ffn2_in[l], w_ffn2_out[l])
        params, bd64 = _layer_params(l, w_in, g_qnorm, g_knorm, w_spatial, b_spatial, ln_g, ln_b,
                                     g_q_a, w_uq, g_kv_a, w_ukv)
        wo = w_out[l].reshape(N_MIXERS, -1, d).astype(BF16)
        a_extra = (lam_vecs[l], g_subln[l].reshape(-1, 1))
        lam0 = _lambda_init(l)

        x_ctx = _ffn(x_ctx, mod_ctx[l], g_pre[l], g_post[l], ffn1, j=0, **ctxk)
        x_lat = _ffn(x_lat, mod_lat[l], g_pre[l], g_post[l], ffn1, j=0, **lat)

        pc = _mix_in(x_ctx, mod_ctx[l], g_pre[l], params, None, seq=n_ctx, **ctxk)
        pl_ = _mix_in(x_lat, mod_lat[l], g_pre[l], params, tables, seq=seq, **lat)
        qa_c, ka_c, va_c, qb_c, kb_c, vb_c, cc_c, qd_c, kd_c, vd_c = pc
        qa_l, ka_l, va_l, qb_l, kb_l, vb_l, cc_l, qd_l, kd_l, vd_l = pl_

        a_l = _attn("a", qa_l, [(ka_l, va_l), (ka_c, va_c)], a_extra, n_batch=n_batch, lam_init=lam0)
        b_l = _attn("b", qb_l, [(kb_l, vb_l), (kb_c, vb_c)], (), n_batch=n_batch)
        d_l = _attn("d", qd_l, [(kd_l, vd_l), (kd_c, vd_c)], (), n_batch=n_batch)
        x_lat = _ffn(x_lat, mod_lat[l], g_pre[l], g_post[l], ffn2, j=2,
                     mixer=((a_l, b_l, cc_l, d_l), wo), **lat)

        if need_ctx:
            a_c = _attn("a", qa_c, [(ka_c, va_c)], a_extra, n_batch=n_batch, lam_init=lam0)
            b_c = _attn("b", qb_c, [(kb_c, vb_c)], (), n_batch=n_batch)
            d_c = _attn("d", qd_c, [(kd_c, vd_c)], (), n_batch=n_batch)
            x_ctx = _ffn(x_ctx, mod_ctx[l], g_pre[l], g_post[l], ffn2, j=2,
                         mixer=((a_c, b_c, cc_c, d_c), wo), **ctxk)

    return x_lat.reshape(n_batch, seq, d)
```

```python
import functools
import math

import jax
import jax.numpy as jnp
import numpy as np
from jax import lax
from jax.experimental import pallas as pl
from jax.experimental.pallas import tpu as pltpu

F32 = jnp.float32
BF16 = jnp.bfloat16

GRID_W = 64
CHUNK = 128
ROPE_THETA = 10000.0
NORM_EPS = 1e-6
FFN_RES = 0.5
N_MOD = 9
N_MIXERS = 4
A_HEADS, B_HEADS, B_KV_HEADS, C_GROUPS, D_HEADS = 4, 4, 2, 4, 4

LANES = 128
MXU_DIM = 256
VMEM_LIMIT_BYTES = 58 * 2**20

ROW_TILE = 1024
FF_TILE = 1024
MIX_SUB_ROWS = 1024


def _dot(a, b):
    return jnp.dot(a, b, preferred_element_type=F32)


def _dot_nt(a, b):
    return lax.dot_general(a, b, (((1,), (1,)), ((), ())), preferred_element_type=F32)


def _split_bf16(x):
    hi = x.astype(BF16)
    lo = (x - hi.astype(F32)).astype(BF16)
    return hi, lo


def _group_mean(xsq, bd_ref):
    hi, lo = _split_bf16(xsq)
    bd = bd_ref[...]
    return _dot(hi, bd) + _dot(lo, bd)


def _rms(x):
    return x * lax.rsqrt(jnp.mean(x * x, axis=-1, keepdims=True) + NORM_EPS)


def _pre_norm(x, mod_ref, gpre_ref, j):
    shift = mod_ref[0, 3 * j:3 * j + 1, :]
    gain = gpre_ref[j:j + 1, :] * (1.0 + mod_ref[0, 3 * j + 1:3 * j + 2, :])
    return _rms(x) * gain + shift


def _post_norm(x, y, mod_ref, gpost_ref, j, res):
    gain = (res * mod_ref[0, 3 * j + 2:3 * j + 3, :]) * gpost_ref[j:j + 1, :]
    return x + _rms(y) * gain


def _gelu_tanh(x):
    c = math.sqrt(2.0 / math.pi)
    return x * (0.5 * (1.0 + jnp.tanh(c * (x + 0.044715 * (x * x * x)))))


def _compiler_params(n_axes):
    return pltpu.CompilerParams(dimension_semantics=("arbitrary",) * n_axes,
                                vmem_limit_bytes=VMEM_LIMIT_BYTES)


def _resident(shape):
    nd = len(shape)
    return pl.BlockSpec(shape, lambda *_: (0,) * nd, pipeline_mode=pl.Buffered(1))


def _ada_kernel(c_ref, w_ref, b_ref, o_ref):
    c = c_ref[...]
    s = c * jax.nn.sigmoid(c)
    s_hi, s_lo = _split_bf16(s)
    w_hi, w_lo = _split_bf16(w_ref[0])
    o_ref[0] = _dot(s_hi, w_hi) + (_dot(s_hi, w_lo) + _dot(s_lo, w_hi)) + b_ref[0]


def _ada(cc, w_ada, b_ada):
    n_layers, d, n = w_ada.shape
    rows = cc.shape[0]
    tn = 1024
    return pl.pallas_call(
        _ada_kernel,
        grid=(n_layers, n // tn),
        in_specs=[pl.BlockSpec((rows, d), lambda l, j: (0, 0)),
                  pl.BlockSpec((1, d, tn), lambda l, j: (l, 0, j)),
                  pl.BlockSpec((1, 1, tn), lambda l, j: (l, 0, j))],
        out_specs=pl.BlockSpec((1, rows, tn), lambda l, j: (l, 0, j)),
        out_shape=jax.ShapeDtypeStruct((n_layers, rows, n), F32),
        compiler_params=_compiler_params(2),
        name="ada",
    )(cc, w_ada, b_ada.reshape(n_layers, 1, n))


def _ffn_kernel(*refs, j, n_mix):
    x_ref, mod_ref, gpre_ref, gpost_ref, w1_ref, w2_ref = refs[:6]
    mix_refs = refs[6:6 + n_mix]
    o_ref, xn_scr, acc_scr = refs[6 + n_mix:]
    d_ff = w2_ref.shape[0]

    def swiglu_chunk(xn, wg, wu, wd):
        hg = _dot(xn, wg)
        hu = _dot(xn, wu)
        return _dot(((hg * jax.nn.sigmoid(hg)) * hu).astype(BF16), wd)

    x = x_ref[...]
    if n_mix:
        wout_ref = mix_refs[-1]
        y = _dot(mix_refs[0][...], wout_ref[0])
        for g in range(1, n_mix - 1):
            y += _dot(mix_refs[g][...], wout_ref[g])
        x = _post_norm(x, y, mod_ref, gpost_ref, 1, 1.0)

    first = d_ff % FF_TILE or FF_TILE
    bounds = [0] + list(range(first, d_ff + 1, FF_TILE))
    chunks = list(zip(bounds[:-1], bounds[1:]))
    n_rows = x.shape[0]
    halves = [slice(0, n_rows // 2), slice(n_rows // 2, n_rows)]

    def chunk(rows, a, b):
        return swiglu_chunk(xn_scr[rows, :], w1_ref[:, a:b], w1_ref[:, d_ff + a:d_ff + b],
                            w2_ref[a:b, :])

    for rows in halves:
        xn_scr[rows, :] = _pre_norm(x[rows], mod_ref, gpre_ref, j).astype(BF16)
        acc_scr[rows, :] = chunk(rows, *chunks[0])
    for a, b in chunks[1:-1]:
        acc_scr[...] += chunk(slice(None), a, b)
    for rows in halves:
        y = acc_scr[rows, :] + chunk(rows, *chunks[-1]) if len(chunks) > 1 else acc_scr[rows, :]
        o_ref[rows, :] = _post_norm(x[rows], y, mod_ref, gpost_ref, j, FFN_RES)


def _ffn(x, mod, g_pre, g_post, weights, *, j, rows_per_mod, mixer=None):
    rows, d = x.shape
    tm = min(ROW_TILE, rows_per_mod)
    per_mod = rows_per_mod // tm
    mix_args, mix_specs = [], []
    if mixer is not None:
        groups, w_out = mixer
        mix_args = list(groups) + [w_out]
        mix_specs = ([pl.BlockSpec((tm, g.shape[1]), lambda i: (i, 0)) for g in groups]
                     + [_resident(w_out.shape)])
    return pl.pallas_call(
        functools.partial(_ffn_kernel, j=j, n_mix=len(mix_args)),
        grid=(rows // tm,),
        in_specs=[pl.BlockSpec((tm, d), lambda i: (i, 0)),
                  pl.BlockSpec((1, N_MOD, d), lambda i: (i // per_mod, 0, 0)),
                  _resident(g_pre.shape), _resident(g_post.shape)]
                 + [_resident(w.shape) for w in weights] + mix_specs,
        out_specs=pl.BlockSpec((tm, d), lambda i: (i, 0)),
        out_shape=jax.ShapeDtypeStruct((rows, d), F32),
        scratch_shapes=[pltpu.VMEM((tm, d), BF16), pltpu.VMEM((tm, d), F32)],
        compiler_params=_compiler_params(1),
        name=f"ffn{j}" if mixer is None else f"mix_out_ffn{j}",
    )(x, mod, g_pre, g_post, *weights, *mix_args)


_IN_BLOCKS = ("aq", "ak", "av", "bq", "bk", "bv", "cu", "cv", "dq", "dkr", "dkv")
_IN_WIDTHS = dict(aq=256, ak=256, av=256, bq=256, bk=256, bv=256, cu=256, cv=256,
                  dq=256, dkr=128, dkv=128)
_IN_OFFSETS = {}
_off = 0
for _n in _IN_BLOCKS:
    _IN_OFFSETS[_n] = _off
    _off += _IN_WIDTHS[_n]
IN_EXT_WIDTH = _off


def _rope_slabs(x, tabs, shift, store):
    cos, sin = tabs
    if cos is not None:
        lane = lax.broadcasted_iota(jnp.int32, cos.shape, 1)
        first = (lane & (2 * shift - 1)) < shift
    for s in range(x.shape[1] // LANES):
        xs = x[:, s * LANES:(s + 1) * LANES]
        if cos is not None:
            partner = jnp.where(first, pltpu.roll(xs, LANES - shift, axis=1),
                                pltpu.roll(xs, shift, axis=1))
            xs = xs * cos + partner * sin
        store(s, xs.astype(BF16))


def _mix_in_kernel(*refs, rope):
    (x_ref, mod_ref, gpre_ref, win_ref, gq_ref, gk_ref, bd64_ref, lng_ref, lnb_ref, wsp_ref,
     bsp_ref, gqa_ref, wuq_ref, gkva_ref, wukvk_ref, wukvv_ref) = refs[:16]
    refs = refs[16:]
    tab_refs = refs[:6] if rope else ()
    refs = refs[6:] if rope else refs
    qa_ref, ka_ref, va_ref, qb_ref, kb_ref, vb_ref, cc_ref, qd_ref, kd_ref, vd_ref = refs
    sub = min(MIX_SUB_ROWS, x_ref.shape[0])

    def sub_block(r, carry):
        rows = pl.ds(pl.multiple_of(r * sub, sub), sub)
        if rope:
            tabs = [t[rows, :] for t in tab_refs]
            t32, t64, td = tabs[0:2], tabs[2:4], tabs[4:6]
        else:
            t32 = t64 = td = (None, None)

        xn = _pre_norm(x_ref[rows, :], mod_ref, gpre_ref, 1).astype(BF16)

        def proj(name):
            o = _IN_OFFSETS[name]
            return _dot(xn, win_ref[:, o:o + _IN_WIDTHS[name]])

        def slab_store(ref):
            def store(s, v):
                ref[rows, s * LANES:(s + 1) * LANES] = v
            return store

        _rope_slabs(proj("aq"), t32, 8, slab_store(qa_ref))
        _rope_slabs(proj("ak"), t32, 8, slab_store(ka_ref))
        va_ref[rows, :] = proj("av").astype(BF16)

        def qk_norm(p, g_ref):
            return p * lax.rsqrt(_group_mean(p * p, bd64_ref) + NORM_EPS) * g_ref[...]

        _rope_slabs(qk_norm(proj("bq"), gq_ref), t64, 16, slab_store(qb_ref))
        _rope_slabs(qk_norm(proj("bk"), gk_ref), t64, 16, slab_store(kb_ref))
        vb_ref[rows, :] = proj("bv").astype(BF16)

        u = _gelu_tanh(proj("cu"))
        v = _gelu_tanh(proj("cv"))
        mu = jnp.mean(v, axis=-1, keepdims=True)
        vc = v - mu
        vln = (vc * lax.rsqrt(jnp.mean(vc * vc, axis=-1, keepdims=True) + NORM_EPS) * lng_ref[...]
               + lnb_ref[...]).astype(BF16)
        width = vln.shape[1]
        gw = width // C_GROUPS
        lane = lax.broadcasted_iota(jnp.int32, (CHUNK, width), 1)
        wsp = wsp_ref[...]
        bsp = bsp_ref[...]
        gated = []
        for ci in range(sub // CHUNK):
            crows = slice(ci * CHUNK, (ci + 1) * CHUNK)
            allg = _dot(wsp, vln[crows, :])
            mixed = allg[0:CHUNK]
            for g in range(1, C_GROUPS):
                mixed = jnp.where(lane >= g * gw, allg[g * CHUNK:(g + 1) * CHUNK], mixed)
            gated.append((u[crows, :] * (mixed + bsp)).astype(BF16))
        cc_ref[rows, :] = jnp.concatenate(gated, axis=0)

        cq = (_rms(proj("dq")) * gqa_ref[...]).astype(BF16)
        _rope_slabs(_dot(cq, wuq_ref[...]), td, 8, slab_store(qd_ref))
        ckv = (_rms(proj("dkv")) * gkva_ref[...]).astype(BF16)
        kr = proj("dkr")
        _rope_slabs(_dot(ckv, wukvk_ref[...]) + jnp.concatenate([kr] * D_HEADS, axis=1), td, 8,
                    slab_store(kd_ref))
        vd_ref[rows, :] = _dot(ckv, wukvv_ref[...]).astype(BF16)
        return carry

    lax.fori_loop(0, x_ref.shape[0] // sub, sub_block, 0)


def _mix_in(x, mod, g_pre, params, tables, *, rows_per_mod, seq):
    rows, d = x.shape
    tm = min(ROW_TILE, rows_per_mod)
    per_mod = rows_per_mod // tm
    rope = tables is not None
    n_mod_blocks = rows // rows_per_mod
    if rope:
        grid = (seq // tm, n_mod_blocks)
        row_map = lambda t, b: (b * per_mod + t, 0)
        mod_map = lambda t, b: (b, 0, 0)
        tab_specs = [pl.BlockSpec((tm, LANES), lambda t, b: (t, 0)) for _ in tables]
    else:
        grid = (rows // tm, 1)
        row_map = lambda i, _: (i, 0)
        mod_map = lambda i, _: (i // per_mod, 0, 0)
        tab_specs = []
    out_widths = (256, 256, 256, 256, 256, 256, 256, 512, 512, 256)
    return pl.pallas_call(
        functools.partial(_mix_in_kernel, rope=rope),
        grid=grid,
        in_specs=[pl.BlockSpec((tm, d), row_map), pl.BlockSpec((1, N_MOD, d), mod_map),
                  _resident(g_pre.shape)] + [_resident(p.shape) for p in params] + tab_specs,
        out_specs=[pl.BlockSpec((tm, w), row_map) for w in out_widths],
        out_shape=[jax.ShapeDtypeStruct((rows, w), BF16) for w in out_widths],
        compiler_params=_compiler_params(2),
        name="mix_in_rope" if rope else "mix_in",
    )(x, mod, g_pre, *params, *(tables or ()))


def _attn_groups(kind):
    if kind == "a":
        return [(0, [(h * 64 + m * 32, h * 64 + (m + 1) * 32, h) for h in (2 * g, 2 * g + 1)
                     for m in (0, 1)]) for g in range(2)]
    if kind == "b":
        return [(0, [(h * 64, (h + 1) * 64, h) for h in (2 * g, 2 * g + 1)]) for g in range(2)]
    if kind == "d":
        return [(g, [(i * LANES, (i + 1) * LANES, 2 * g + i) for i in range(2)]) for g in range(2)]
    raise ValueError(kind)


ATTN_Q_TILE = dict(a=256, b=512, d=256)
HEAD_W = 64
VT_ROWS = 80
ATTN_KEY_CHUNKS = 2


def _key_chunks(n_keys):
    tiles, ragged = divmod(n_keys, MXU_DIM)
    if ragged or tiles < ATTN_KEY_CHUNKS:
        half = n_keys // 2
        return [(0, half), (half, n_keys)]
    base, extra = divmod(tiles, ATTN_KEY_CHUNKS)
    bounds = [0]
    for c in range(ATTN_KEY_CHUNKS):
        bounds.append(bounds[-1] + (base + (c < extra)) * MXU_DIM)
    return list(zip(bounds[:-1], bounds[1:]))


def _attn_kernel(*refs, groups, n_src, tq, pair, lam_init):
    q_ref = refs[0]
    src = refs[1:1 + 2 * n_src]
    refs = refs[1 + 2 * n_src:]
    if pair:
        lam_ref, gsub_ref = refs[:2]
        refs = refs[2:]
    o_ref, k_scr, vt_scr, s_a, s_b = refs

    off = 0
    for i in range(n_src):
        n = src[2 * i].shape[0]
        k_scr[off:off + n, :] = src[2 * i][...]
        for r in range(0, n, MXU_DIM):
            blk = src[2 * i + 1][r:r + MXU_DIM, :].astype(F32)
            blk_t = blk.T.astype(BF16)
            for h in range(blk_t.shape[0] // HEAD_W):
                vt_scr[h * VT_ROWS:h * VT_ROWS + HEAD_W, off + r:off + r + blk.shape[0]] = (
                    blk_t[h * HEAD_W:(h + 1) * HEAD_W])
        off += n
    n_keys = off
    for h in range(vt_scr.shape[0] // VT_ROWS):
        vt_scr[h * VT_ROWS + HEAD_W:(h + 1) * VT_ROWS, :] = jnp.ones((VT_ROWS - HEAD_W, n_keys), BF16)
    chunks = [slice(a, b) for a, b in _key_chunks(n_keys)]
    bufs = (s_a, s_b)
    n_tiles = q_ref.shape[0] // tq
    lane_q = lax.broadcasted_iota(jnp.int32, (tq, MXU_DIM), 1)

    if pair:
        lv = lam_ref[...]
        lam = (jnp.exp(jnp.sum(lv[0:1] * lv[1:2], axis=-1, keepdims=True))
               - jnp.exp(jnp.sum(lv[2:3] * lv[3:4], axis=-1, keepdims=True)) + lam_init)

    def tile_rows(i):
        if isinstance(i, int):
            return slice(i * tq, (i + 1) * tq)
        return pl.ds(pl.multiple_of(i * tq, tq), tq)

    def scores_t(group, i, k_rows, s_dst):
        slab, maps = group
        qcols = slice(slab * MXU_DIM, (slab + 1) * MXU_DIM)
        qg = q_ref[tile_rows(i), qcols]
        zero = jnp.zeros_like(qg)
        qs = jnp.concatenate([jnp.where((lane_q >= lo) & (lane_q < hi), qg, zero)
                              for lo, hi, _ in maps], axis=0)
        s_dst[...] = _dot_nt(k_scr[k_rows, qcols], qs)

    def q_tile(group, i, prefetch):
        _, maps = group
        heads = sorted({h for _, _, h in maps})

        def weigh(s, m, k_cols):
            pb = jnp.exp2(s - m).astype(BF16)
            return [_dot(vt_scr[h * VT_ROWS:(h + 1) * VT_ROWS, k_cols], pb[:, j * tq:(j + 1) * tq])
                    for j, (_, _, h) in enumerate(maps)]

        m = acc = None
        for c, k_cols in enumerate(chunks):
            if c + 1 < len(chunks):
                nxt = chunks[c + 1]
                scores_t(group, i, nxt, bufs[(c + 1) % 2].at[:nxt.stop - nxt.start])
            elif prefetch is not None:
                prefetch()
            s = bufs[c % 2][:k_cols.stop - k_cols.start, :]
            cm = jnp.max(s, axis=0, keepdims=True)
            if m is None:
                m = cm
                acc = weigh(s, m, k_cols)
            else:
                m_new = jnp.maximum(m, cm)
                alpha = jnp.exp2(m - m_new)
                acc_c = weigh(s, m_new, k_cols)
                acc = [a * alpha[:, j * tq:(j + 1) * tq] + a_c
                       for j, (a, a_c) in enumerate(zip(acc, acc_c))]
                m = m_new

        def out_t(j):
            return acc[j][:HEAD_W] * (1.0 / acc[j][HEAD_W:HEAD_W + 1])

        rows = []
        for h in heads:
            js = [j for j, (_, _, hh) in enumerate(maps) if hh == h]
            if pair:
                o = out_t(js[0]) - lam * out_t(js[1])
                o = (o * lax.rsqrt(jnp.mean(o * o, axis=0, keepdims=True) + NORM_EPS)
                     * gsub_ref[...] * (1.0 - lam_init))
            else:
                o = out_t(js[0])
            rows.append(o)
        out = jnp.concatenate(rows, axis=0).T
        ocols = slice(heads[0] * HEAD_W, (heads[-1] + 1) * HEAD_W)
        o_ref[tile_rows(i), ocols] = out.astype(BF16)

    first = s_a.at[:chunks[0].stop]
    scores_t(groups[0], 0, chunks[0], first)
    for gi, group in enumerate(groups):
        def body(i, carry, group=group):
            q_tile(group, i, lambda: scores_t(group, i + 1, chunks[0], first))
            return carry

        if n_tiles > 1:
            lax.fori_loop(0, n_tiles - 1, body, 0)
        if gi + 1 < len(groups):
            q_tile(group, n_tiles - 1,
                   lambda nxt=groups[gi + 1]: scores_t(nxt, 0, chunks[0], first))
        else:
            q_tile(group, n_tiles - 1, None)


def _attn(kind, q, sources, extra, *, n_batch, lam_init=0.0):
    groups = _attn_groups(kind)
    pair = kind == "a"
    rows, qw = q.shape
    sq = rows // n_batch
    tq = min(ATTN_Q_TILE[kind], sq)
    n_keys = sum(k.shape[0] // n_batch for k, _ in sources)
    out_w = sources[0][1].shape[1]
    cols = max(len(maps) for _, maps in groups) * tq
    in_specs = [pl.BlockSpec((sq, qw), lambda b: (b, 0))]
    args = [q]
    for k, v in sources:
        n = k.shape[0] // n_batch
        in_specs += [pl.BlockSpec((n, k.shape[1]), lambda b: (b, 0)),
                     pl.BlockSpec((n, v.shape[1]), lambda b: (b, 0))]
        args += [k, v]
    if pair:
        in_specs += [_resident(e.shape) for e in extra]
        args += list(extra)
    assert n_keys % (2 * LANES) == 0 and sq % tq == 0
    sizes = [b - a for a, b in _key_chunks(n_keys)]
    return pl.pallas_call(
        functools.partial(_attn_kernel, groups=groups, n_src=len(sources), tq=tq, pair=pair,
                          lam_init=lam_init),
        grid=(n_batch,),
        in_specs=in_specs,
        out_specs=pl.BlockSpec((sq, out_w), lambda b: (b, 0)),
        out_shape=jax.ShapeDtypeStruct((rows, out_w), BF16),
        scratch_shapes=[pltpu.VMEM((n_keys, qw), BF16),
                        pltpu.VMEM((out_w // HEAD_W * VT_ROWS, n_keys), BF16),
                        pltpu.VMEM((max(sizes[0::2]), cols), F32),
                        pltpu.VMEM((max(sizes[1::2]), cols), F32)],
        compiler_params=_compiler_params(1),
        name=f"attn_{kind}_{len(sources)}",
    )(*args)


def _rope_table(seq, pattern):
    n_rows = seq // GRID_W
    n_freq = next(e[2] for e in pattern if e is not None)
    axis_dim = 2 * n_freq
    inv_freq = ROPE_THETA ** (-jnp.arange(0, axis_dim, 2, dtype=F32) / axis_dim)
    active = np.array([e is not None for e in pattern])
    axis = np.array([e[0] if e is not None else 0 for e in pattern])
    freq = np.array([e[1] if e is not None else 0 for e in pattern])
    first = np.array([bool(e[3]) if e is not None else False for e in pattern])

    def per_position(n, fn):
        return fn(jnp.arange(n, dtype=F32)[:, None] * inv_freq[freq][None, :])

    def full(fn):
        by_row = jnp.repeat(per_position(n_rows, fn), GRID_W, axis=0)
        by_col = jnp.tile(per_position(GRID_W, fn), (n_rows, 1))
        return jnp.where(axis[None, :] == 0, by_row, by_col)

    cos = jnp.where(active[None, :], full(jnp.cos), 1.0)
    sin = jnp.where(active[None, :], full(jnp.sin), 0.0)
    return cos, jnp.where(first[None, :], -sin, sin)


def _rope_pattern(rot_dim):
    n_freq = rot_dim // 4
    return [(r // (2 * n_freq), r % n_freq, n_freq, (r % (2 * n_freq)) < n_freq)
            for r in range(rot_dim)]


def _rope_tables(seq):
    p32 = _rope_pattern(32) * (LANES // 32)
    p64 = _rope_pattern(64) * (LANES // 64)
    pd = [None] * 64 + _rope_pattern(32) + [None] * 32
    return _rope_table(seq, p32) + _rope_table(seq, p64) + _rope_table(seq, pd)


def _layer_params(l, w_in, g_qnorm, g_knorm, w_spatial, b_spatial, ln_g, ln_b, g_q_a, w_uq,
                  g_kv_a, w_ukv):
    d_model = w_in.shape[1]
    sizes = (256, 256, 256, 256, 128, 128, 256, 256, 256, 128, 32)
    offs = [0]
    for s in sizes:
        offs.append(offs[-1] + s)
    aq, ak, av, bq, bk, bv, cu, cv, dq, dkv, dkr = (w_in[l][:, offs[i]:offs[i + 1]]
                                                    for i in range(11))
    log2e = math.log2(math.e)
    a_scale = 32 ** -0.5 * log2e
    b_scale = 64 ** -0.5 * log2e
    d_scale = 96 ** -0.5 * log2e

    def rep_kv(w):
        hd = w.shape[1] // B_KV_HEADS
        n_rep = B_HEADS // B_KV_HEADS
        return jnp.concatenate([w[:, (h // n_rep) * hd:(h // n_rep + 1) * hd]
                                for h in range(B_HEADS)], axis=1)

    zeros = lambda n: jnp.zeros((d_model, n), w_in.dtype)
    dkr_placed = jnp.concatenate([zeros(64), dkr, zeros(32)], axis=1)
    blocks = dict(aq=aq * a_scale, ak=ak, av=av, bq=bq, bk=rep_kv(bk), bv=rep_kv(bv), cu=cu, cv=cv,
                  dq=dq, dkr=dkr_placed, dkv=dkv)
    w_ext = jnp.concatenate([blocks[n] for n in _IN_BLOCKS], axis=1).astype(BF16)

    def tile_row(v, n, scale=1.0):
        return (jnp.tile(v, n) * scale).reshape(1, -1)

    bd64 = jnp.kron(jnp.eye(4, dtype=F32), jnp.full((64, 64), 1.0 / 64, F32)).astype(BF16)
    wsp = w_spatial[l].reshape(C_GROUPS * CHUNK, CHUNK).astype(BF16)
    bsp = jnp.repeat(b_spatial[l].T, 256 // C_GROUPS, axis=1)

    uq = w_uq[l].reshape(-1, D_HEADS, 96) * d_scale
    uq = jnp.concatenate([uq, jnp.zeros(uq.shape[:2] + (32,), uq.dtype)], axis=-1)
    wuq = uq.reshape(uq.shape[0], D_HEADS * LANES).astype(BF16)
    ukv = w_ukv[l].reshape(-1, D_HEADS, 128)
    ukv_k = jnp.concatenate([ukv[..., :64], jnp.zeros(ukv.shape[:2] + (64,), ukv.dtype)], axis=-1)
    wukvk = ukv_k.reshape(ukv.shape[0], D_HEADS * LANES).astype(BF16)
    wukvv = ukv[..., 64:].reshape(ukv.shape[0], D_HEADS * 64).astype(BF16)

    return (w_ext, tile_row(g_qnorm[l], B_HEADS, b_scale), tile_row(g_knorm[l], B_HEADS), bd64,
            ln_g[l].reshape(1, -1), ln_b[l].reshape(1, -1), wsp, bsp,
            g_q_a[l].reshape(1, -1), wuq, g_kv_a[l].reshape(1, -1), wukvk, wukvv), bd64


def _ffn_weights(w_in, w_out):
    return w_in.astype(BF16), w_out.astype(BF16)


def _lambda_init(layer_idx):
    return 0.8 - 0.6 * math.exp(-0.3 * layer_idx)


def kernel(x, c, ctx, c_ctx, w_ada, b_ada, g_pre, g_post, w_ffn1_in, w_ffn1_out, w_ffn2_in,
           w_ffn2_out, w_in, w_out, lam_vecs, g_subln, g_qnorm, g_knorm, w_spatial, b_spatial,
           ln_g, ln_b, g_q_a, w_uq, g_kv_a, w_ukv):
    n_batch, seq, d = x.shape
    n_ctx = ctx.shape[1]
    depth = w_ada.shape[0]

    n_cond = n_batch + 1
    pad = (-n_cond) % 8
    cc = jnp.concatenate([c, c_ctx[None, :], jnp.zeros((pad, d), c.dtype)], axis=0)
    mod = _ada(cc, w_ada, b_ada)
    mod_lat = mod[:, :n_batch].reshape(depth, n_batch, N_MOD, d)
    mod_ctx = mod[:, n_batch:n_cond].reshape(depth, 1, N_MOD, d)

    tables = _rope_tables(seq)
    x_lat = x.reshape(n_batch * seq, d)
    x_ctx = ctx.reshape(n_batch * n_ctx, d)
    lat = dict(rows_per_mod=seq)
    ctxk = dict(rows_per_mod=n_batch * n_ctx)

    for l in range(depth):
        need_ctx = l < depth - 1
        ffn1 = _ffn_weights(w_ffn1_in[l], w_ffn1_out[l])
        ffn2 = _ffn_weights(w_ffn2_in[l], w_ffn2_out[l])
        params, bd64 = _layer_params(l, w_in, g_qnorm, g_knorm, w_spatial, b_spatial, ln_g, ln_b,
                                     g_q_a, w_uq, g_kv_a, w_ukv)
        wo = w_out[l].reshape(N_MIXERS, -1, d).astype(BF16)
        a_extra = (lam_vecs[l], g_subln[l].reshape(-1, 1))
        lam0 = _lambda_init(l)

        x_ctx = _ffn(x_ctx, mod_ctx[l], g_pre[l], g_post[l], ffn1, j=0, **ctxk)
        x_lat = _ffn(x_lat, mod_lat[l], g_pre[l], g_post[l], ffn1, j=0, **lat)

        pc = _mix_in(x_ctx, mod_ctx[l], g_pre[l], params, None, seq=n_ctx, **ctxk)
        pl_ = _mix_in(x_lat, mod_lat[l], g_pre[l], params, tables, seq=seq, **lat)
        qa_c, ka_c, va_c, qb_c, kb_c, vb_c, cc_c, qd_c, kd_c, vd_c = pc
        qa_l, ka_l, va_l, qb_l, kb_l, vb_l, cc_l, qd_l, kd_l, vd_l = pl_

        a_l = _attn("a", qa_l, [(ka_l, va_l), (ka_c, va_c)], a_extra, n_batch=n_batch, lam_init=lam0)
        b_l = _attn("b", qb_l, [(kb_l, vb_l), (kb_c, vb_c)], (), n_batch=n_batch)
        d_l = _attn("d", qd_l, [(kd_l, vd_l), (kd_c, vd_c)], (), n_batch=n_batch)
        x_lat = _ffn(x_lat, mod_lat[l], g_pre[l], g_post[l], ffn2, j=2,
                     mixer=((a_l, b_l, cc_l, d_l), wo), **lat)

        if need_ctx:
            a_c = _attn("a", qa_c, [(ka_c, va_c)], a_extra, n_batch=n_batch, lam_init=lam0)
            b_c = _attn("b", qb_c, [(kb_c, vb_c)], (), n_batch=n_batch)
            d_c = _attn("d", qd_c, [(kd_c, vd_c)], (), n_batch=n_batch)
            x_ctx = _ffn(x_ctx, mod_ctx[l], g_pre[l], g_post[l], ffn2, j=2,
                         mixer=((a_c, b_c, cc_c, d_c), wo), **ctxk)

    return x_lat.reshape(n_batch, seq, d)
```

```python
import functools
import math

import jax
import jax.numpy as jnp
import numpy as np
from jax import lax
from jax.experimental import pallas as pl
from jax.experimental.pallas import tpu as pltpu

F32 = jnp.float32
BF16 = jnp.bfloat16

GRID_W = 64
CHUNK = 128
ROPE_THETA = 10000.0
NORM_EPS = 1e-6
FFN_RES = 0.5
N_MOD = 9
N_MIXERS = 4
B_HEADS, B_KV_HEADS, C_GROUPS, D_HEADS = 4, 2, 4, 4

LANES = 128
MXU_DIM = 256
VMEM_LIMIT_BYTES = 58 * 2**20

ROW_TILE = 1024
FF_TILE = 1024
MIX_SUB_ROWS = 1024


def _dot(a, b):
    return jnp.dot(a, b, preferred_element_type=F32)


def _dot_nt(a, b):
    return lax.dot_general(a, b, (((1,), (1,)), ((), ())), preferred_element_type=F32)


def _split_bf16(x):
    hi = x.astype(BF16)
    lo = (x - hi.astype(F32)).astype(BF16)
    return hi, lo


def _group_mean(xsq, bd_ref):
    hi, lo = _split_bf16(xsq)
    bd = bd_ref[...]
    return _dot(hi, bd) + _dot(lo, bd)


def _rms(x):
    return x * lax.rsqrt(jnp.mean(x * x, axis=-1, keepdims=True) + NORM_EPS)


def _pre_norm(x, mod_ref, gpre_ref, j):
    shift = mod_ref[0, 3 * j:3 * j + 1, :]
    gain = gpre_ref[j:j + 1, :] * (1.0 + mod_ref[0, 3 * j + 1:3 * j + 2, :])
    return _rms(x) * gain + shift


def _post_norm(x, y, mod_ref, gpost_ref, j, res):
    gain = (res * mod_ref[0, 3 * j + 2:3 * j + 3, :]) * gpost_ref[j:j + 1, :]
    return x + _rms(y) * gain


def _gelu_tanh(x):
    c = math.sqrt(2.0 / math.pi)
    return x * (0.5 * (1.0 + jnp.tanh(c * (x + 0.044715 * (x * x * x)))))


def _compiler_params(n_axes):
    return pltpu.CompilerParams(dimension_semantics=("arbitrary",) * n_axes,
                                vmem_limit_bytes=VMEM_LIMIT_BYTES)


def _resident(shape):
    nd = len(shape)
    return pl.BlockSpec(shape, lambda *_: (0,) * nd, pipeline_mode=pl.Buffered(1))


def _ada_kernel(c_ref, w_ref, b_ref, o_ref):
    c = c_ref[...]
    s = c * jax.nn.sigmoid(c)
    s_hi, s_lo = _split_bf16(s)
    w_hi, w_lo = _split_bf16(w_ref[0])
    o_ref[0] = _dot(s_hi, w_hi) + (_dot(s_hi, w_lo) + _dot(s_lo, w_hi)) + b_ref[0]


def _ada(cc, w_ada, b_ada):
    n_layers, d, n = w_ada.shape
    rows = cc.shape[0]
    tn = 1024
    return pl.pallas_call(
        _ada_kernel,
        grid=(n_layers, n // tn),
        in_specs=[pl.BlockSpec((rows, d), lambda l, j: (0, 0)),
                  pl.BlockSpec((1, d, tn), lambda l, j: (l, 0, j)),
                  pl.BlockSpec((1, 1, tn), lambda l, j: (l, 0, j))],
        out_specs=pl.BlockSpec((1, rows, tn), lambda l, j: (l, 0, j)),
        out_shape=jax.ShapeDtypeStruct((n_layers, rows, n), F32),
        compiler_params=_compiler_params(2),
        name="ada",
    )(cc, w_ada, b_ada.reshape(n_layers, 1, n))


def _ffn_kernel(*refs, j, n_mix):
    x_ref, mod_ref, gpre_ref, gpost_ref, w1_ref, w2_ref = refs[:6]
    mix_refs = refs[6:6 + n_mix]
    o_ref, xn_scr, acc_scr = refs[6 + n_mix:]
    d_ff = w2_ref.shape[0]

    def swiglu_chunk(xn, wg, wu, wd):
        hg = _dot(xn, wg)
        hu = _dot(xn, wu)
        return _dot(((hg * jax.nn.sigmoid(hg)) * hu).astype(BF16), wd)

    x = x_ref[...]
    if n_mix:
        wout_ref = mix_refs[-1]
        y = _dot(mix_refs[0][...], wout_ref[0])
        for g in range(1, n_mix - 1):
            y += _dot(mix_refs[g][...], wout_ref[g])
        x = _post_norm(x, y, mod_ref, gpost_ref, 1, 1.0)

    first = d_ff % FF_TILE or FF_TILE
    bounds = [0] + list(range(first, d_ff + 1, FF_TILE))
    chunks = list(zip(bounds[:-1], bounds[1:]))
    n_rows = x.shape[0]
    halves = [slice(0, n_rows // 2), slice(n_rows // 2, n_rows)]

    def chunk(rows, a, b):
        return swiglu_chunk(xn_scr[rows, :], w1_ref[:, a:b], w1_ref[:, d_ff + a:d_ff + b],
                            w2_ref[a:b, :])

    for rows in halves:
        xn_scr[rows, :] = _pre_norm(x[rows], mod_ref, gpre_ref, j).astype(BF16)
        acc_scr[rows, :] = chunk(rows, *chunks[0])
    for a, b in chunks[1:-1]:
        acc_scr[...] += chunk(slice(None), a, b)
    for rows in halves:
        y = acc_scr[rows, :] + chunk(rows, *chunks[-1]) if len(chunks) > 1 else acc_scr[rows, :]
        o_ref[rows, :] = _post_norm(x[rows], y, mod_ref, gpost_ref, j, FFN_RES)


def _ffn(x, mod, g_pre, g_post, weights, *, j, rows_per_mod, mixer=None):
    rows, d = x.shape
    tm = min(ROW_TILE, rows_per_mod)
    per_mod = rows_per_mod // tm
    mix_args, mix_specs = [], []
    if mixer is not None:
        groups, w_out = mixer
        mix_args = list(groups) + [w_out]
        mix_specs = ([pl.BlockSpec((tm, g.shape[1]), lambda i: (i, 0)) for g in groups]
                     + [_resident(w_out.shape)])
    return pl.pallas_call(
        functools.partial(_ffn_kernel, j=j, n_mix=len(mix_args)),
        grid=(rows // tm,),
        in_specs=[pl.BlockSpec((tm, d), lambda i: (i, 0)),
                  pl.BlockSpec((1, N_MOD, d), lambda i: (i // per_mod, 0, 0)),
                  _resident(g_pre.shape), _resident(g_post.shape)]
                 + [_resident(w.shape) for w in weights] + mix_specs,
        out_specs=pl.BlockSpec((tm, d), lambda i: (i, 0)),
        out_shape=jax.ShapeDtypeStruct((rows, d), F32),
        scratch_shapes=[pltpu.VMEM((tm, d), BF16), pltpu.VMEM((tm, d), F32)],
        compiler_params=_compiler_params(1),
        name=f"ffn{j}" if mixer is None else f"mix_out_ffn{j}",
    )(x, mod, g_pre, g_post, *weights, *mix_args)


_IN_BLOCKS = ("aq", "ak", "av", "bq", "bk", "bv", "cu", "cv", "dq", "dkr", "dkv")
_IN_WIDTHS = dict(aq=256, ak=256, av=256, bq=256, bk=256, bv=256, cu=256, cv=256,
                  dq=256, dkr=128, dkv=128)
_IN_OFFSETS = {}
_off = 0
for _n in _IN_BLOCKS:
    _IN_OFFSETS[_n] = _off
    _off += _IN_WIDTHS[_n]


def _rope_slabs(x, tabs, shift, store):
    cos, sin = tabs
    if cos is not None:
        lane = lax.broadcasted_iota(jnp.int32, cos.shape, 1)
        first = (lane & (2 * shift - 1)) < shift
    for s in range(x.shape[1] // LANES):
        xs = x[:, s * LANES:(s + 1) * LANES]
        if cos is not None:
            partner = jnp.where(first, pltpu.roll(xs, LANES - shift, axis=1),
                                pltpu.roll(xs, shift, axis=1))
            xs = xs * cos + partner * sin
        store(s, xs.astype(BF16))


def _mix_in_kernel(*refs, rope):
    (x_ref, mod_ref, gpre_ref, win_ref, gq_ref, gk_ref, bd64_ref, lng_ref, lnb_ref, wsp_ref,
     bsp_ref, gqa_ref, wuq_ref, gkva_ref, wukvk_ref, wukvv_ref) = refs[:16]
    refs = refs[16:]
    tab_refs = refs[:6] if rope else ()
    refs = refs[6:] if rope else refs
    qa_ref, ka_ref, va_ref, qb_ref, kb_ref, vb_ref, cc_ref, qd_ref, kd_ref, vd_ref = refs
    sub = min(MIX_SUB_ROWS, x_ref.shape[0])

    def sub_block(r, carry):
        rows = pl.ds(pl.multiple_of(r * sub, sub), sub)
        if rope:
            tabs = [t[rows, :] for t in tab_refs]
            t32, t64, td = tabs[0:2], tabs[2:4], tabs[4:6]
        else:
            t32 = t64 = td = (None, None)

        xn = _pre_norm(x_ref[rows, :], mod_ref, gpre_ref, 1).astype(BF16)

        def proj(name):
            o = _IN_OFFSETS[name]
            return _dot(xn, win_ref[:, o:o + _IN_WIDTHS[name]])

        def slab_store(ref):
            def store(s, v):
                ref[rows, s * LANES:(s + 1) * LANES] = v
            return store

        _rope_slabs(proj("aq"), t32, 8, slab_store(qa_ref))
        _rope_slabs(proj("ak"), t32, 8, slab_store(ka_ref))
        va_ref[rows, :] = proj("av").astype(BF16)

        def qk_norm(p, g_ref):
            return p * lax.rsqrt(_group_mean(p * p, bd64_ref) + NORM_EPS) * g_ref[...]

        _rope_slabs(qk_norm(proj("bq"), gq_ref), t64, 16, slab_store(qb_ref))
        _rope_slabs(qk_norm(proj("bk"), gk_ref), t64, 16, slab_store(kb_ref))
        vb_ref[rows, :] = proj("bv").astype(BF16)

        u = _gelu_tanh(proj("cu"))
        v = _gelu_tanh(proj("cv"))
        mu = jnp.mean(v, axis=-1, keepdims=True)
        vc = v - mu
        vln = (vc * lax.rsqrt(jnp.mean(vc * vc, axis=-1, keepdims=True) + NORM_EPS) * lng_ref[...]
               + lnb_ref[...]).astype(BF16)
        width = vln.shape[1]
        gw = width // C_GROUPS
        lane = lax.broadcasted_iota(jnp.int32, (CHUNK, width), 1)
        wsp = wsp_ref[...]
        bsp = bsp_ref[...]
        gated = []
        for ci in range(sub // CHUNK):
            crows = slice(ci * CHUNK, (ci + 1) * CHUNK)
            allg = _dot(wsp, vln[crows, :])
            mixed = allg[0:CHUNK]
            for g in range(1, C_GROUPS):
                mixed = jnp.where(lane >= g * gw, allg[g * CHUNK:(g + 1) * CHUNK], mixed)
            gated.append((u[crows, :] * (mixed + bsp)).astype(BF16))
        cc_ref[rows, :] = jnp.concatenate(gated, axis=0)

        cq = (_rms(proj("dq")) * gqa_ref[...]).astype(BF16)
        _rope_slabs(_dot(cq, wuq_ref[...]), td, 8, slab_store(qd_ref))
        ckv = (_rms(proj("dkv")) * gkva_ref[...]).astype(BF16)
        kr = proj("dkr")
        _rope_slabs(_dot(ckv, wukvk_ref[...]) + jnp.concatenate([kr] * D_HEADS, axis=1), td, 8,
                    slab_store(kd_ref))
        vd_ref[rows, :] = _dot(ckv, wukvv_ref[...]).astype(BF16)
        return carry

    lax.fori_loop(0, x_ref.shape[0] // sub, sub_block, 0)


def _mix_in(x, mod, g_pre, params, tables, *, rows_per_mod, seq):
    rows, d = x.shape
    tm = min(ROW_TILE, rows_per_mod)
    per_mod = rows_per_mod // tm
    rope = tables is not None
    n_mod_blocks = rows // rows_per_mod
    if rope:
        grid = (seq // tm, n_mod_blocks)
        row_map = lambda t, b: (b * per_mod + t, 0)
        mod_map = lambda t, b: (b, 0, 0)
        tab_specs = [pl.BlockSpec((tm, LANES), lambda t, b: (t, 0)) for _ in tables]
    else:
        grid = (rows // tm, 1)
        row_map = lambda i, _: (i, 0)
        mod_map = lambda i, _: (i // per_mod, 0, 0)
        tab_specs = []
    out_widths = (256, 256, 256, 256, 256, 256, 256, 512, 512, 256)
    return pl.pallas_call(
        functools.partial(_mix_in_kernel, rope=rope),
        grid=grid,
        in_specs=[pl.BlockSpec((tm, d), row_map), pl.BlockSpec((1, N_MOD, d), mod_map),
                  _resident(g_pre.shape)] + [_resident(p.shape) for p in params] + tab_specs,
        out_specs=[pl.BlockSpec((tm, w), row_map) for w in out_widths],
        out_shape=[jax.ShapeDtypeStruct((rows, w), BF16) for w in out_widths],
        compiler_params=_compiler_params(2),
        name="mix_in_rope" if rope else "mix_in",
    )(x, mod, g_pre, *params, *(tables or ()))


def _attn_groups(kind):
    if kind == "a":
        return [(0, [(h * 64 + m * 32, h * 64 + (m + 1) * 32, h) for h in (2 * g, 2 * g + 1)
                     for m in (0, 1)]) for g in range(2)]
    if kind == "b":
        return [(0, [(h * 64, (h + 1) * 64, h) for h in (2 * g, 2 * g + 1)]) for g in range(2)]
    if kind == "d":
        return [(g, [(i * LANES, (i + 1) * LANES, 2 * g + i) for i in range(2)]) for g in range(2)]
    raise ValueError(kind)


ATTN_Q_TILE = dict(a=256, b=512, d=256)
HEAD_W = 64
VT_ROWS = 80
ATTN_KEY_CHUNKS = 2


def _key_chunks(n_keys):
    tiles, ragged = divmod(n_keys, MXU_DIM)
    if ragged or tiles < ATTN_KEY_CHUNKS:
        half = n_keys // 2
        return [(0, half), (half, n_keys)]
    base, extra = divmod(tiles, ATTN_KEY_CHUNKS)
    bounds = [0]
    for c in range(ATTN_KEY_CHUNKS):
        bounds.append(bounds[-1] + (base + (c < extra)) * MXU_DIM)
    return list(zip(bounds[:-1], bounds[1:]))


def _attn_kernel(*refs, groups, n_src, tq, pair, lam_init):
    q_ref = refs[0]
    src = refs[1:1 + 2 * n_src]
    refs = refs[1 + 2 * n_src:]
    if pair:
        lam_ref, gsub_ref = refs[:2]
        refs = refs[2:]
    o_ref, k_scr, vt_scr, s_a, s_b = refs

    off = 0
    for i in range(n_src):
        n = src[2 * i].shape[0]
        k_scr[off:off + n, :] = src[2 * i][...]
        for r in range(0, n, MXU_DIM):
            blk = src[2 * i + 1][r:r + MXU_DIM, :].astype(F32)
            blk_t = blk.T.astype(BF16)
            for h in range(blk_t.shape[0] // HEAD_W):
                vt_scr[h * VT_ROWS:h * VT_ROWS + HEAD_W, off + r:off + r + blk.shape[0]] = (
                    blk_t[h * HEAD_W:(h + 1) * HEAD_W])
        off += n
    n_keys = off
    for h in range(vt_scr.shape[0] // VT_ROWS):
        vt_scr[h * VT_ROWS + HEAD_W:(h + 1) * VT_ROWS, :] = jnp.ones((VT_ROWS - HEAD_W, n_keys), BF16)
    chunks = [slice(a, b) for a, b in _key_chunks(n_keys)]
    bufs = (s_a, s_b)
    n_tiles = q_ref.shape[0] // tq
    lane_q = lax.broadcasted_iota(jnp.int32, (tq, MXU_DIM), 1)

    if pair:
        lv = lam_ref[...]
        lam = (jnp.exp(jnp.sum(lv[0:1] * lv[1:2], axis=-1, keepdims=True))
               - jnp.exp(jnp.sum(lv[2:3] * lv[3:4], axis=-1, keepdims=True)) + lam_init)

    def tile_rows(i):
        if isinstance(i, int):
            return slice(i * tq, (i + 1) * tq)
        return pl.ds(pl.multiple_of(i * tq, tq), tq)

    def scores_t(group, i, k_rows, s_dst):
        slab, maps = group
        qcols = slice(slab * MXU_DIM, (slab + 1) * MXU_DIM)
        qg = q_ref[tile_rows(i), qcols]
        zero = jnp.zeros_like(qg)
        qs = jnp.concatenate([jnp.where((lane_q >= lo) & (lane_q < hi), qg, zero)
                              for lo, hi, _ in maps], axis=0)
        s_dst[...] = _dot_nt(k_scr[k_rows, qcols], qs)

    def q_tile(group, i, prefetch):
        _, maps = group
        heads = sorted({h for _, _, h in maps})

        def weigh(s, m, k_cols):
            pb = jnp.exp2(s - m).astype(BF16)
            return [_dot(vt_scr[h * VT_ROWS:(h + 1) * VT_ROWS, k_cols], pb[:, j * tq:(j + 1) * tq])
                    for j, (_, _, h) in enumerate(maps)]

        m = acc = None
        for c, k_cols in enumerate(chunks):
            if c + 1 < len(chunks):
                nxt = chunks[c + 1]
                scores_t(group, i, nxt, bufs[(c + 1) % 2].at[:nxt.stop - nxt.start])
            elif prefetch is not None:
                prefetch()
            s = bufs[c % 2][:k_cols.stop - k_cols.start, :]
            cm = jnp.max(s, axis=0, keepdims=True)
            if m is None:
                m = cm
                acc = weigh(s, m, k_cols)
            else:
                m_new = jnp.maximum(m, cm)
                alpha = jnp.exp2(m - m_new)
                acc_c = weigh(s, m_new, k_cols)
                acc = [a * alpha[:, j * tq:(j + 1) * tq] + a_c
                       for j, (a, a_c) in enumerate(zip(acc, acc_c))]
                m = m_new

        def out_t(j):
            return acc[j][:HEAD_W] * (1.0 / acc[j][HEAD_W:HEAD_W + 1])

        rows = []
        for h in heads:
            js = [j for j, (_, _, hh) in enumerate(maps) if hh == h]
            if pair:
                o = out_t(js[0]) - lam * out_t(js[1])
                o = (o * lax.rsqrt(jnp.mean(o * o, axis=0, keepdims=True) + NORM_EPS)
                     * gsub_ref[...] * (1.0 - lam_init))
            else:
                o = out_t(js[0])
            rows.append(o)
        out = jnp.concatenate(rows, axis=0).T
        ocols = slice(heads[0] * HEAD_W, (heads[-1] + 1) * HEAD_W)
        o_ref[tile_rows(i), ocols] = out.astype(BF16)

    first = s_a.at[:chunks[0].stop]
    scores_t(groups[0], 0, chunks[0], first)
    for gi, group in enumerate(groups):
        def body(i, carry, group=group):
            q_tile(group, i, lambda: scores_t(group, i + 1, chunks[0], first))
            return carry

        if n_tiles > 1:
            lax.fori_loop(0, n_tiles - 1, body, 0)
        if gi + 1 < len(groups):
            q_tile(group, n_tiles - 1,
                   lambda nxt=groups[gi + 1]: scores_t(nxt, 0, chunks[0], first))
        else:
            q_tile(group, n_tiles - 1, None)


def _attn(kind, q, sources, extra, *, n_batch, lam_init=0.0):
    groups = _attn_groups(kind)
    pair = kind == "a"
    rows, qw = q.shape
    sq = rows // n_batch
    tq = min(ATTN_Q_TILE[kind], sq)
    n_keys = sum(k.shape[0] // n_batch for k, _ in sources)
    out_w = sources[0][1].shape[1]
    cols = max(len(maps) for _, maps in groups) * tq
    in_specs = [pl.BlockSpec((sq, qw), lambda b: (b, 0))]
    args = [q]
    for k, v in sources:
        n = k.shape[0] // n_batch
        in_specs += [pl.BlockSpec((n, k.shape[1]), lambda b: (b, 0)),
                     pl.BlockSpec((n, v.shape[1]), lambda b: (b, 0))]
        args += [k, v]
    if pair:
        in_specs += [_resident(e.shape) for e in extra]
        args += list(extra)
    assert n_keys % (2 * LANES) == 0 and sq % tq == 0
    sizes = [b - a for a, b in _key_chunks(n_keys)]
    return pl.pallas_call(
        functools.partial(_attn_kernel, groups=groups, n_src=len(sources), tq=tq, pair=pair,
                          lam_init=lam_init),
        grid=(n_batch,),
        in_specs=in_specs,
        out_specs=pl.BlockSpec((sq, out_w), lambda b: (b, 0)),
        out_shape=jax.ShapeDtypeStruct((rows, out_w), BF16),
        scratch_shapes=[pltpu.VMEM((n_keys, qw), BF16),
                        pltpu.VMEM((out_w // HEAD_W * VT_ROWS, n_keys), BF16),
                        pltpu.VMEM((max(sizes[0::2]), cols), F32),
                        pltpu.VMEM((max(sizes[1::2]), cols), F32)],
        compiler_params=_compiler_params(1),
        name=f"attn_{kind}_{len(sources)}",
    )(*args)


def _rope_table(seq, pattern):
    n_rows = seq // GRID_W
    n_freq = next(e[2] for e in pattern if e is not None)
    axis_dim = 2 * n_freq
    inv_freq = ROPE_THETA ** (-jnp.arange(0, axis_dim, 2, dtype=F32) / axis_dim)
    active = np.array([e is not None for e in pattern])
    axis = np.array([e[0] if e is not None else 0 for e in pattern])
    freq = np.array([e[1] if e is not None else 0 for e in pattern])
    first = np.array([bool(e[3]) if e is not None else False for e in pattern])

    def per_position(n, fn):
        return fn(jnp.arange(n, dtype=F32)[:, None] * inv_freq[freq][None, :])

    def full(fn):
        by_row = jnp.repeat(per_position(n_rows, fn), GRID_W, axis=0)
        by_col = jnp.tile(per_position(GRID_W, fn), (n_rows, 1))
        return jnp.where(axis[None, :] == 0, by_row, by_col)

    cos = jnp.where(active[None, :], full(jnp.cos), 1.0)
    sin = jnp.where(active[None, :], full(jnp.sin), 0.0)
    return cos, jnp.where(first[None, :], -sin, sin)


def _rope_pattern(rot_dim):
    n_freq = rot_dim // 4
    return [(r // (2 * n_freq), r % n_freq, n_freq, (r % (2 * n_freq)) < n_freq)
            for r in range(rot_dim)]


def _rope_tables(seq):
    p32 = _rope_pattern(32) * (LANES // 32)
    p64 = _rope_pattern(64) * (LANES // 64)
    pd = [None] * 64 + _rope_pattern(32) + [None] * 32
    return _rope_table(seq, p32) + _rope_table(seq, p64) + _rope_table(seq, pd)


def _layer_params(l, w_in, g_qnorm, g_knorm, w_spatial, b_spatial, ln_g, ln_b, g_q_a, w_uq,
                  g_kv_a, w_ukv):
    d_model = w_in.shape[1]
    sizes = (256, 256, 256, 256, 128, 128, 256, 256, 256, 128, 32)
    offs = [0]
    for s in sizes:
        offs.append(offs[-1] + s)
    aq, ak, av, bq, bk, bv, cu, cv, dq, dkv, dkr = (w_in[l][:, offs[i]:offs[i + 1]]
                                                    for i in range(11))
    log2e = math.log2(math.e)
    a_scale = 32 ** -0.5 * log2e
    b_scale = 64 ** -0.5 * log2e
    d_scale = 96 ** -0.5 * log2e

    def rep_kv(w):
        hd = w.shape[1] // B_KV_HEADS
        n_rep = B_HEADS // B_KV_HEADS
        return jnp.concatenate([w[:, (h // n_rep) * hd:(h // n_rep + 1) * hd]
                                for h in range(B_HEADS)], axis=1)

    zeros = lambda n: jnp.zeros((d_model, n), w_in.dtype)
    dkr_placed = jnp.concatenate([zeros(64), dkr, zeros(32)], axis=1)
    blocks = dict(aq=aq * a_scale, ak=ak, av=av, bq=bq, bk=rep_kv(bk), bv=rep_kv(bv), cu=cu, cv=cv,
                  dq=dq, dkr=dkr_placed, dkv=dkv)
    w_ext = jnp.concatenate([blocks[n] for n in _IN_BLOCKS], axis=1).astype(BF16)

    def tile_row(v, n, scale=1.0):
        return (jnp.tile(v, n) * scale).reshape(1, -1)

    bd64 = jnp.kron(jnp.eye(4, dtype=F32), jnp.full((64, 64), 1.0 / 64, F32)).astype(BF16)
    wsp = w_spatial[l].reshape(C_GROUPS * CHUNK, CHUNK).astype(BF16)
    bsp = jnp.repeat(b_spatial[l].T, 256 // C_GROUPS, axis=1)

    uq = w_uq[l].reshape(-1, D_HEADS, 96) * d_scale
    uq = jnp.concatenate([uq, jnp.zeros(uq.shape[:2] + (32,), uq.dtype)], axis=-1)
    wuq = uq.reshape(uq.shape[0], D_HEADS * LANES).astype(BF16)
    ukv = w_ukv[l].reshape(-1, D_HEADS, 128)
    ukv_k = jnp.concatenate([ukv[..., :64], jnp.zeros(ukv.shape[:2] + (64,), ukv.dtype)], axis=-1)
    wukvk = ukv_k.reshape(ukv.shape[0], D_HEADS * LANES).astype(BF16)
    wukvv = ukv[..., 64:].reshape(ukv.shape[0], D_HEADS * 64).astype(BF16)

    return (w_ext, tile_row(g_qnorm[l], B_HEADS, b_scale), tile_row(g_knorm[l], B_HEADS), bd64,
            ln_g[l].reshape(1, -1), ln_b[l].reshape(1, -1), wsp, bsp,
            g_q_a[l].reshape(1, -1), wuq, g_kv_a[l].reshape(1, -1), wukvk, wukvv)


def _ffn_weights(w_in, w_out):
    return w_in.astype(BF16), w_out.astype(BF16)


def _lambda_init(layer_idx):
    return 0.8 - 0.6 * math.exp(-0.3 * layer_idx)


def kernel(x, c, ctx, c_ctx, w_ada, b_ada, g_pre, g_post, w_ffn1_in, w_ffn1_out, w_ffn2_in,
           w_ffn2_out, w_in, w_out, lam_vecs, g_subln, g_qnorm, g_knorm, w_spatial, b_spatial,
           ln_g, ln_b, g_q_a, w_uq, g_kv_a, w_ukv):
    n_batch, seq, d = x.shape
    n_ctx = ctx.shape[1]
    depth = w_ada.shape[0]

    n_cond = n_batch + 1
    pad = (-n_cond) % 8
    cc = jnp.concatenate([c, c_ctx[None, :], jnp.zeros((pad, d), c.dtype)], axis=0)
    mod = _ada(cc, w_ada, b_ada)
    mod_lat = mod[:, :n_batch].reshape(depth, n_batch, N_MOD, d)
    mod_ctx = mod[:, n_batch:n_cond].reshape(depth, 1, N_MOD, d)

    tables = _rope_tables(seq)
    x_lat = x.reshape(n_batch * seq, d)
    x_ctx = ctx.reshape(n_batch * n_ctx, d)
    lat = dict(rows_per_mod=seq)
    ctxk = dict(rows_per_mod=n_batch * n_ctx)

    for l in range(depth):
        need_ctx = l < depth - 1
        ffn1 = _ffn_weights(w_ffn1_in[l], w_ffn1_out[l])
        ffn2 = _ffn_weights(w_ffn2_in[l], w_ffn2_out[l])
        params = _layer_params(l, w_in, g_qnorm, g_knorm, w_spatial, b_spatial, ln_g, ln_b,
                               g_q_a, w_uq, g_kv_a, w_ukv)
        wo = w_out[l].reshape(N_MIXERS, -1, d).astype(BF16)
        a_extra = (lam_vecs[l], g_subln[l].reshape(-1, 1))
        lam0 = _lambda_init(l)

        x_ctx = _ffn(x_ctx, mod_ctx[l], g_pre[l], g_post[l], ffn1, j=0, **ctxk)
        x_lat = _ffn(x_lat, mod_lat[l], g_pre[l], g_post[l], ffn1, j=0, **lat)

        pc = _mix_in(x_ctx, mod_ctx[l], g_pre[l], params, None, seq=n_ctx, **ctxk)
        pl_ = _mix_in(x_lat, mod_lat[l], g_pre[l], params, tables, seq=seq, **lat)
        qa_c, ka_c, va_c, qb_c, kb_c, vb_c, cc_c, qd_c, kd_c, vd_c = pc
        qa_l, ka_l, va_l, qb_l, kb_l, vb_l, cc_l, qd_l, kd_l, vd_l = pl_

        a_l = _attn("a", qa_l, [(ka_l, va_l), (ka_c, va_c)], a_extra, n_batch=n_batch, lam_init=lam0)
        b_l = _attn("b", qb_l, [(kb_l, vb_l), (kb_c, vb_c)], (), n_batch=n_batch)
        d_l = _attn("d", qd_l, [(kd_l, vd_l), (kd_c, vd_c)], (), n_batch=n_batch)
        x_lat = _ffn(x_lat, mod_lat[l], g_pre[l], g_post[l], ffn2, j=2,
                     mixer=((a_l, b_l, cc_l, d_l), wo), **lat)

        if need_ctx:
            a_c = _attn("a", qa_c, [(ka_c, va_c)], a_extra, n_batch=n_batch, lam_init=lam0)
            b_c = _attn("b", qb_c, [(kb_c, vb_c)], (), n_batch=n_batch)
            d_c = _attn("d", qd_c, [(kd_c, vd_c)], (), n_batch=n_batch)
            x_ctx = _ffn(x_ctx, mod_ctx[l], g_pre[l], g_post[l], ffn2, j=2,
                         mixer=((a_c, b_c, cc_c, d_c), wo), **ctxk)

    return x_lat.reshape(n_batch, seq, d)
```

```python
import functools
import math

import jax
import jax.numpy as jnp
import numpy as np
from jax import lax
from jax.experimental import pallas as pl
from jax.experimental.pallas import tpu as pltpu

F32 = jnp.float32
BF16 = jnp.bfloat16

GRID_W = 64
CHUNK = 128
ROPE_THETA = 10000.0
NORM_EPS = 1e-6
FFN_RES = 0.5
N_MOD = 9
N_MIXERS = 4
B_HEADS, B_KV_HEADS, C_GROUPS, D_HEADS = 4, 2, 4, 4

LANES = 128
MXU_DIM = 256
VMEM_LIMIT_BYTES = 62 * 2**20

ROW_TILE = 1024
FF_TILE = 1024
MIX_SUB_ROWS = 1024


def _dot(a, b):
    return jnp.dot(a, b, preferred_element_type=F32)


def _dot_nt(a, b):
    return lax.dot_general(a, b, (((1,), (1,)), ((), ())), preferred_element_type=F32)


def _split_bf16(x):
    hi = x.astype(BF16)
    lo = (x - hi.astype(F32)).astype(BF16)
    return hi, lo


def _group_mean(xsq, bd_ref):
    hi, lo = _split_bf16(xsq)
    bd = bd_ref[...]
    return _dot(hi, bd) + _dot(lo, bd)


def _rms(x):
    return x * lax.rsqrt(jnp.mean(x * x, axis=-1, keepdims=True) + NORM_EPS)


def _pre_norm(x, mod_ref, gpre_ref, j):
    shift = mod_ref[0, 3 * j:3 * j + 1, :]
    gain = gpre_ref[j:j + 1, :] * (1.0 + mod_ref[0, 3 * j + 1:3 * j + 2, :])
    return _rms(x) * gain + shift


def _post_norm(x, y, mod_ref, gpost_ref, j, res):
    gain = (res * mod_ref[0, 3 * j + 2:3 * j + 3, :]) * gpost_ref[j:j + 1, :]
    return x + _rms(y) * gain


def _gelu_tanh(x):
    c = math.sqrt(2.0 / math.pi)
    return x * (0.5 * (1.0 + jnp.tanh(c * (x + 0.044715 * (x * x * x)))))


def _compiler_params(n_axes):
    return pltpu.CompilerParams(dimension_semantics=("arbitrary",) * n_axes,
                                vmem_limit_bytes=VMEM_LIMIT_BYTES)


def _resident(shape):
    nd = len(shape)
    return pl.BlockSpec(shape, lambda *_: (0,) * nd, pipeline_mode=pl.Buffered(1))


def _ada_kernel(c_ref, w_ref, b_ref, o_ref):
    c = c_ref[...]
    s = c * jax.nn.sigmoid(c)
    s_hi, s_lo = _split_bf16(s)
    w_hi, w_lo = _split_bf16(w_ref[0])
    o_ref[0] = _dot(s_hi, w_hi) + (_dot(s_hi, w_lo) + _dot(s_lo, w_hi)) + b_ref[0]


def _ada(cc, w_ada, b_ada):
    n_layers, d, n = w_ada.shape
    rows = cc.shape[0]
    tn = 1024
    return pl.pallas_call(
        _ada_kernel,
        grid=(n_layers, n // tn),
        in_specs=[pl.BlockSpec((rows, d), lambda l, j: (0, 0)),
                  pl.BlockSpec((1, d, tn), lambda l, j: (l, 0, j)),
                  pl.BlockSpec((1, 1, tn), lambda l, j: (l, 0, j))],
        out_specs=pl.BlockSpec((1, rows, tn), lambda l, j: (l, 0, j)),
        out_shape=jax.ShapeDtypeStruct((n_layers, rows, n), F32),
        compiler_params=_compiler_params(2),
        name="ada",
    )(cc, w_ada, b_ada.reshape(n_layers, 1, n))


def _ffn_kernel(*refs, j, n_mix):
    x_ref, mod_ref, gpre_ref, gpost_ref, w1_ref, w2_ref = refs[:6]
    mix_refs = refs[6:6 + n_mix]
    o_ref, xn_scr, acc_scr = refs[6 + n_mix:]
    d_ff = w2_ref.shape[0]

    def swiglu_chunk(xn, wg, wu, wd):
        hg = _dot(xn, wg)
        hu = _dot(xn, wu)
        return _dot(((hg * jax.nn.sigmoid(hg)) * hu).astype(BF16), wd)

    x = x_ref[...]
    if n_mix:
        wout_ref = mix_refs[-1]
        y = _dot(mix_refs[0][...], wout_ref[0])
        for g in range(1, n_mix - 1):
            y += _dot(mix_refs[g][...], wout_ref[g])
        x = _post_norm(x, y, mod_ref, gpost_ref, 1, 1.0)

    first = d_ff % FF_TILE or FF_TILE
    bounds = [0] + list(range(first, d_ff + 1, FF_TILE))
    chunks = list(zip(bounds[:-1], bounds[1:]))
    n_rows = x.shape[0]
    halves = [slice(0, n_rows // 2), slice(n_rows // 2, n_rows)]

    def chunk(rows, a, b):
        return swiglu_chunk(xn_scr[rows, :], w1_ref[:, a:b], w1_ref[:, d_ff + a:d_ff + b],
                            w2_ref[a:b, :])

    for rows in halves:
        xn_scr[rows, :] = _pre_norm(x[rows], mod_ref, gpre_ref, j).astype(BF16)
        acc_scr[rows, :] = chunk(rows, *chunks[0])
    for a, b in chunks[1:-1]:
        acc_scr[...] += chunk(slice(None), a, b)
    for rows in halves:
        y = acc_scr[rows, :] + chunk(rows, *chunks[-1]) if len(chunks) > 1 else acc_scr[rows, :]
        o_ref[rows, :] = _post_norm(x[rows], y, mod_ref, gpost_ref, j, FFN_RES)


def _ffn(x, mod, g_pre, g_post, weights, *, j, rows_per_mod, mixer=None):
    rows, d = x.shape
    tm = min(ROW_TILE, rows_per_mod)
    per_mod = rows_per_mod // tm
    mix_args, mix_specs = [], []
    if mixer is not None:
        groups, w_out = mixer
        mix_args = list(groups) + [w_out]
        mix_specs = ([pl.BlockSpec((tm, g.shape[1]), lambda i: (i, 0)) for g in groups]
                     + [_resident(w_out.shape)])
    return pl.pallas_call(
        functools.partial(_ffn_kernel, j=j, n_mix=len(mix_args)),
        grid=(rows // tm,),
        in_specs=[pl.BlockSpec((tm, d), lambda i: (i, 0)),
                  pl.BlockSpec((1, N_MOD, d), lambda i: (i // per_mod, 0, 0)),
                  _resident(g_pre.shape), _resident(g_post.shape)]
                 + [_resident(w.shape) for w in weights] + mix_specs,
        out_specs=pl.BlockSpec((tm, d), lambda i: (i, 0)),
        out_shape=jax.ShapeDtypeStruct((rows, d), F32),
        scratch_shapes=[pltpu.VMEM((tm, d), BF16), pltpu.VMEM((tm, d), F32)],
        compiler_params=_compiler_params(1),
        name=f"ffn{j}" if mixer is None else f"mix_out_ffn{j}",
    )(x, mod, g_pre, g_post, *weights, *mix_args)


_IN_BLOCKS = ("aq", "ak", "av", "bq", "bk", "bv", "cu", "cv", "dq", "dkr", "dkv")
_IN_WIDTHS = dict(aq=256, ak=256, av=256, bq=256, bk=256, bv=256, cu=256, cv=256,
                  dq=256, dkr=128, dkv=128)
_IN_OFFSETS = {}
_off = 0
for _n in _IN_BLOCKS:
    _IN_OFFSETS[_n] = _off
    _off += _IN_WIDTHS[_n]


def _rope_slabs(x, tabs, shift, store):
    cos, sin = tabs
    if cos is not None:
        lane = lax.broadcasted_iota(jnp.int32, cos.shape, 1)
        first = (lane & (2 * shift - 1)) < shift
    for s in range(x.shape[1] // LANES):
        xs = x[:, s * LANES:(s + 1) * LANES]
        if cos is not None:
            partner = jnp.where(first, pltpu.roll(xs, LANES - shift, axis=1),
                                pltpu.roll(xs, shift, axis=1))
            xs = xs * cos + partner * sin
        store(s, xs.astype(BF16))


def _mix_in_kernel(*refs, rope):
    (x_ref, mod_ref, gpre_ref, win_ref, gq_ref, gk_ref, bd64_ref, lng_ref, lnb_ref, wsp_ref,
     bsp_ref, gqa_ref, wuq_ref, gkva_ref, wukvk_ref, wukvv_ref) = refs[:16]
    refs = refs[16:]
    tab_refs = refs[:6] if rope else ()
    refs = refs[6:] if rope else refs
    qa_ref, ka_ref, va_ref, qb_ref, kb_ref, vb_ref, cc_ref, qd_ref, kd_ref, vd_ref = refs
    sub = min(MIX_SUB_ROWS, x_ref.shape[0])

    def sub_block(r, carry):
        rows = pl.ds(pl.multiple_of(r * sub, sub), sub)
        if rope:
            tabs = [t[rows, :] for t in tab_refs]
            t32, t64, td = tabs[0:2], tabs[2:4], tabs[4:6]
        else:
            t32 = t64 = td = (None, None)

        xn = _pre_norm(x_ref[rows, :], mod_ref, gpre_ref, 1).astype(BF16)

        def proj(name):
            o = _IN_OFFSETS[name]
            return _dot(xn, win_ref[:, o:o + _IN_WIDTHS[name]])

        def slab_store(ref):
            def store(s, v):
                ref[rows, s * LANES:(s + 1) * LANES] = v
            return store

        _rope_slabs(proj("aq"), t32, 8, slab_store(qa_ref))
        _rope_slabs(proj("ak"), t32, 8, slab_store(ka_ref))
        va_ref[rows, :] = proj("av").astype(BF16)

        def qk_norm(p, g_ref):
            return p * lax.rsqrt(_group_mean(p * p, bd64_ref) + NORM_EPS) * g_ref[...]

        _rope_slabs(qk_norm(proj("bq"), gq_ref), t64, 16, slab_store(qb_ref))
        _rope_slabs(qk_norm(proj("bk"), gk_ref), t64, 16, slab_store(kb_ref))
        vb_ref[rows, :] = proj("bv").astype(BF16)

        u = _gelu_tanh(proj("cu"))
        v = _gelu_tanh(proj("cv"))
        mu = jnp.mean(v, axis=-1, keepdims=True)
        vc = v - mu
        vln = (vc * lax.rsqrt(jnp.mean(vc * vc, axis=-1, keepdims=True) + NORM_EPS) * lng_ref[...]
               + lnb_ref[...]).astype(BF16)
        width = vln.shape[1]
        gw = width // C_GROUPS
        lane = lax.broadcasted_iota(jnp.int32, (CHUNK, width), 1)
        wsp = wsp_ref[...]
        bsp = bsp_ref[...]
        gated = []
        for ci in range(sub // CHUNK):
            crows = slice(ci * CHUNK, (ci + 1) * CHUNK)
            allg = _dot(wsp, vln[crows, :])
            mixed = allg[0:CHUNK]
            for g in range(1, C_GROUPS):
                mixed = jnp.where(lane >= g * gw, allg[g * CHUNK:(g + 1) * CHUNK], mixed)
            gated.append((u[crows, :] * (mixed + bsp)).astype(BF16))
        cc_ref[rows, :] = jnp.concatenate(gated, axis=0)

        cq = (_rms(proj("dq")) * gqa_ref[...]).astype(BF16)
        _rope_slabs(_dot(cq, wuq_ref[...]), td, 8, slab_store(qd_ref))
        ckv = (_rms(proj("dkv")) * gkva_ref[...]).astype(BF16)
        kr = proj("dkr")
        _rope_slabs(_dot(ckv, wukvk_ref[...]) + jnp.concatenate([kr] * D_HEADS, axis=1), td, 8,
                    slab_store(kd_ref))
        vd_ref[rows, :] = _dot(ckv, wukvv_ref[...]).astype(BF16)
        return carry

    lax.fori_loop(0, x_ref.shape[0] // sub, sub_block, 0)


def _mix_in(x, mod, g_pre, params, tables, *, rows_per_mod, seq):
    rows, d = x.shape
    tm = min(ROW_TILE, rows_per_mod)
    per_mod = rows_per_mod // tm
    rope = tables is not None
    n_mod_blocks = rows // rows_per_mod
    if rope:
        grid = (seq // tm, n_mod_blocks)
        row_map = lambda t, b: (b * per_mod + t, 0)
        mod_map = lambda t, b: (b, 0, 0)
        tab_specs = [pl.BlockSpec((tm, LANES), lambda t, b: (t, 0)) for _ in tables]
    else:
        grid = (rows // tm, 1)
        row_map = lambda i, _: (i, 0)
        mod_map = lambda i, _: (i // per_mod, 0, 0)
        tab_specs = []
    out_widths = (256, 256, 256, 256, 256, 256, 256, 512, 512, 256)
    return pl.pallas_call(
        functools.partial(_mix_in_kernel, rope=rope),
        grid=grid,
        in_specs=[pl.BlockSpec((tm, d), row_map), pl.BlockSpec((1, N_MOD, d), mod_map),
                  _resident(g_pre.shape)] + [_resident(p.shape) for p in params] + tab_specs,
        out_specs=[pl.BlockSpec((tm, w), row_map) for w in out_widths],
        out_shape=[jax.ShapeDtypeStruct((rows, w), BF16) for w in out_widths],
        compiler_params=_compiler_params(2),
        name="mix_in_rope" if rope else "mix_in",
    )(x, mod, g_pre, *params, *(tables or ()))


def _attn_groups(kind):
    if kind == "a":
        return [(0, [(h * 64 + m * 32, h * 64 + (m + 1) * 32, h) for h in (2 * g, 2 * g + 1)
                     for m in (0, 1)]) for g in range(2)]
    if kind == "b":
        return [(0, [(h * 64, (h + 1) * 64, h) for h in (2 * g, 2 * g + 1)]) for g in range(2)]
    if kind == "d":
        return [(g, [(i * LANES, (i + 1) * LANES, 2 * g + i) for i in range(2)]) for g in range(2)]
    raise ValueError(kind)


ATTN_Q_TILE = dict(a=256, b=512, d=512)
HEAD_W = 64
VT_ROWS = 80
ATTN_KEY_CHUNKS = 2


def _key_chunks(n_keys):
    tiles, ragged = divmod(n_keys, MXU_DIM)
    if ragged or tiles < ATTN_KEY_CHUNKS:
        half = n_keys // 2
        return [(0, half), (half, n_keys)]
    base, extra = divmod(tiles, ATTN_KEY_CHUNKS)
    bounds = [0]
    for c in range(ATTN_KEY_CHUNKS):
        bounds.append(bounds[-1] + (base + (c < extra)) * MXU_DIM)
    return list(zip(bounds[:-1], bounds[1:]))


def _attn_kernel(*refs, groups, n_src, tq, pair, lam_init):
    q_ref = refs[0]
    src = refs[1:1 + 2 * n_src]
    refs = refs[1 + 2 * n_src:]
    if pair:
        lam_ref, gsub_ref = refs[:2]
        refs = refs[2:]
    o_ref, k_scr, vt_scr, s_a, s_b = refs

    off = 0
    for i in range(n_src):
        n = src[2 * i].shape[0]
        k_scr[off:off + n, :] = src[2 * i][...]
        for r in range(0, n, MXU_DIM):
            blk = src[2 * i + 1][r:r + MXU_DIM, :].astype(F32)
            blk_t = blk.T.astype(BF16)
            for h in range(blk_t.shape[0] // HEAD_W):
                vt_scr[h * VT_ROWS:h * VT_ROWS + HEAD_W, off + r:off + r + blk.shape[0]] = (
                    blk_t[h * HEAD_W:(h + 1) * HEAD_W])
        off += n
    n_keys = off
    for h in range(vt_scr.shape[0] // VT_ROWS):
        vt_scr[h * VT_ROWS + HEAD_W:(h + 1) * VT_ROWS, :] = jnp.ones((VT_ROWS - HEAD_W, n_keys), BF16)
    chunks = [slice(a, b) for a, b in _key_chunks(n_keys)]
    bufs = (s_a, s_b)
    n_tiles = q_ref.shape[0] // tq
    lane_q = lax.broadcasted_iota(jnp.int32, (tq, MXU_DIM), 1)

    if pair:
        lv = lam_ref[...]
        lam = (jnp.exp(jnp.sum(lv[0:1] * lv[1:2], axis=-1, keepdims=True))
               - jnp.exp(jnp.sum(lv[2:3] * lv[3:4], axis=-1, keepdims=True)) + lam_init)

    def tile_rows(i):
        if isinstance(i, int):
            return slice(i * tq, (i + 1) * tq)
        return pl.ds(pl.multiple_of(i * tq, tq), tq)

    def scores_t(group, i, k_rows, s_dst):
        slab, maps = group
        qcols = slice(slab * MXU_DIM, (slab + 1) * MXU_DIM)
        qg = q_ref[tile_rows(i), qcols]
        zero = jnp.zeros_like(qg)
        qs = jnp.concatenate([jnp.where((lane_q >= lo) & (lane_q < hi), qg, zero)
                              for lo, hi, _ in maps], axis=0)
        s_dst[...] = _dot_nt(k_scr[k_rows, qcols], qs)

    def q_tile(group, i, prefetch):
        _, maps = group
        heads = sorted({h for _, _, h in maps})

        def weigh(s, m, k_cols):
            pb = jnp.exp2(s - m).astype(BF16)
            return [_dot(vt_scr[h * VT_ROWS:(h + 1) * VT_ROWS, k_cols], pb[:, j * tq:(j + 1) * tq])
                    for j, (_, _, h) in enumerate(maps)]

        m = acc = None
        for c, k_cols in enumerate(chunks):
            if c + 1 < len(chunks):
                nxt = chunks[c + 1]
                scores_t(group, i, nxt, bufs[(c + 1) % 2].at[:nxt.stop - nxt.start])
            elif prefetch is not None:
                prefetch()
            s = bufs[c % 2][:k_cols.stop - k_cols.start, :]
            cm = jnp.max(s, axis=0, keepdims=True)
            if m is None:
                m = cm
                acc = weigh(s, m, k_cols)
            else:
                m_new = jnp.maximum(m, cm)
                alpha = jnp.exp2(m - m_new)
                acc_c = weigh(s, m_new, k_cols)
                acc = [a * alpha[:, j * tq:(j + 1) * tq] + a_c
                       for j, (a, a_c) in enumerate(zip(acc, acc_c))]
                m = m_new

        def out_t(j):
            return acc[j][:HEAD_W] * (1.0 / acc[j][HEAD_W:HEAD_W + 1])

        rows = []
        for h in heads:
            js = [j for j, (_, _, hh) in enumerate(maps) if hh == h]
            if pair:
                o = out_t(js[0]) - lam * out_t(js[1])
                o = (o * lax.rsqrt(jnp.mean(o * o, axis=0, keepdims=True) + NORM_EPS)
                     * gsub_ref[...] * (1.0 - lam_init))
            else:
                o = out_t(js[0])
            rows.append(o)
        out = jnp.concatenate(rows, axis=0).T
        ocols = slice(heads[0] * HEAD_W, (heads[-1] + 1) * HEAD_W)
        o_ref[tile_rows(i), ocols] = out.astype(BF16)

    first = s_a.at[:chunks[0].stop]
    scores_t(groups[0], 0, chunks[0], first)
    for gi, group in enumerate(groups):
        def body(i, carry, group=group):
            q_tile(group, i, lambda: scores_t(group, i + 1, chunks[0], first))
            return carry

        if n_tiles > 1:
            lax.fori_loop(0, n_tiles - 1, body, 0)
        if gi + 1 < len(groups):
            q_tile(group, n_tiles - 1,
                   lambda nxt=groups[gi + 1]: scores_t(nxt, 0, chunks[0], first))
        else:
            q_tile(group, n_tiles - 1, None)


def _attn(kind, q, sources, extra, *, n_batch, lam_init=0.0):
    groups = _attn_groups(kind)
    pair = kind == "a"
    rows, qw = q.shape
    sq = rows // n_batch
    tq = min(ATTN_Q_TILE[kind], sq)
    n_keys = sum(k.shape[0] // n_batch for k, _ in sources)
    out_w = sources[0][1].shape[1]
    cols = max(len(maps) for _, maps in groups) * tq
    in_specs = [pl.BlockSpec((sq, qw), lambda b: (b, 0))]
    args = [q]
    for k, v in sources:
        n = k.shape[0] // n_batch
        in_specs += [pl.BlockSpec((n, k.shape[1]), lambda b: (b, 0)),
                     pl.BlockSpec((n, v.shape[1]), lambda b: (b, 0))]
        args += [k, v]
    if pair:
        in_specs += [_resident(e.shape) for e in extra]
        args += list(extra)
    assert n_keys % (2 * LANES) == 0 and sq % tq == 0
    sizes = [b - a for a, b in _key_chunks(n_keys)]
    return pl.pallas_call(
        functools.partial(_attn_kernel, groups=groups, n_src=len(sources), tq=tq, pair=pair,
                          lam_init=lam_init),
        grid=(n_batch,),
        in_specs=in_specs,
        out_specs=pl.BlockSpec((sq, out_w), lambda b: (b, 0)),
        out_shape=jax.ShapeDtypeStruct((rows, out_w), BF16),
        scratch_shapes=[pltpu.VMEM((n_keys, qw), BF16),
                        pltpu.VMEM((out_w // HEAD_W * VT_ROWS, n_keys), BF16),
                        pltpu.VMEM((max(sizes[0::2]), cols), F32),
                        pltpu.VMEM((max(sizes[1::2]), cols), F32)],
        compiler_params=_compiler_params(1),
        name=f"attn_{kind}_{len(sources)}",
    )(*args)


def _rope_table(seq, pattern):
    n_rows = seq // GRID_W
    n_freq = next(e[2] for e in pattern if e is not None)
    axis_dim = 2 * n_freq
    inv_freq = ROPE_THETA ** (-jnp.arange(0, axis_dim, 2, dtype=F32) / axis_dim)
    active = np.array([e is not None for e in pattern])
    axis = np.array([e[0] if e is not None else 0 for e in pattern])
    freq = np.array([e[1] if e is not None else 0 for e in pattern])
    first = np.array([bool(e[3]) if e is not None else False for e in pattern])

    def per_position(n, fn):
        return fn(jnp.arange(n, dtype=F32)[:, None] * inv_freq[freq][None, :])

    def full(fn):
        by_row = jnp.repeat(per_position(n_rows, fn), GRID_W, axis=0)
        by_col = jnp.tile(per_position(GRID_W, fn), (n_rows, 1))
        return jnp.where(axis[None, :] == 0, by_row, by_col)

    cos = jnp.where(active[None, :], full(jnp.cos), 1.0)
    sin = jnp.where(active[None, :], full(jnp.sin), 0.0)
    return cos, jnp.where(first[None, :], -sin, sin)


def _rope_pattern(rot_dim):
    n_freq = rot_dim // 4
    return [(r // (2 * n_freq), r % n_freq, n_freq, (r % (2 * n_freq)) < n_freq)
            for r in range(rot_dim)]


def _rope_tables(seq):
    p32 = _rope_pattern(32) * (LANES // 32)
    p64 = _rope_pattern(64) * (LANES // 64)
    pd = [None] * 64 + _rope_pattern(32) + [None] * 32
    return _rope_table(seq, p32) + _rope_table(seq, p64) + _rope_table(seq, pd)


def _layer_params(l, w_in, g_qnorm, g_knorm, w_spatial, b_spatial, ln_g, ln_b, g_q_a, w_uq,
                  g_kv_a, w_ukv):
    d_model = w_in.shape[1]
    sizes = (256, 256, 256, 256, 128, 128, 256, 256, 256, 128, 32)
    offs = [0]
    for s in sizes:
        offs.append(offs[-1] + s)
    aq, ak, av, bq, bk, bv, cu, cv, dq, dkv, dkr = (w_in[l][:, offs[i]:offs[i + 1]]
                                                    for i in range(11))
    log2e = math.log2(math.e)
    a_scale = 32 ** -0.5 * log2e
    b_scale = 64 ** -0.5 * log2e
    d_scale = 96 ** -0.5 * log2e

    def rep_kv(w):
        hd = w.shape[1] // B_KV_HEADS
        n_rep = B_HEADS // B_KV_HEADS
        return jnp.concatenate([w[:, (h // n_rep) * hd:(h // n_rep + 1) * hd]
                                for h in range(B_HEADS)], axis=1)

    zeros = lambda n: jnp.zeros((d_model, n), w_in.dtype)
    dkr_placed = jnp.concatenate([zeros(64), dkr, zeros(32)], axis=1)
    blocks = dict(aq=aq * a_scale, ak=ak, av=av, bq=bq, bk=rep_kv(bk), bv=rep_kv(bv), cu=cu, cv=cv,
                  dq=dq, dkr=dkr_placed, dkv=dkv)
    w_ext = jnp.concatenate([blocks[n] for n in _IN_BLOCKS], axis=1).astype(BF16)

    def tile_row(v, n, scale=1.0):
        return (jnp.tile(v, n) * scale).reshape(1, -1)

    bd64 = jnp.kron(jnp.eye(4, dtype=F32), jnp.full((64, 64), 1.0 / 64, F32)).astype(BF16)
    wsp = w_spatial[l].reshape(C_GROUPS * CHUNK, CHUNK).astype(BF16)
    bsp = jnp.repeat(b_spatial[l].T, 256 // C_GROUPS, axis=1)

    uq = w_uq[l].reshape(-1, D_HEADS, 96) * d_scale
    uq = jnp.concatenate([uq, jnp.zeros(uq.shape[:2] + (32,), uq.dtype)], axis=-1)
    wuq = uq.reshape(uq.shape[0], D_HEADS * LANES).astype(BF16)
    ukv = w_ukv[l].reshape(-1, D_HEADS, 128)
    ukv_k = jnp.concatenate([ukv[..., :64], jnp.zeros(ukv.shape[:2] + (64,), ukv.dtype)], axis=-1)
    wukvk = ukv_k.reshape(ukv.shape[0], D_HEADS * LANES).astype(BF16)
    wukvv = ukv[..., 64:].reshape(ukv.shape[0], D_HEADS * 64).astype(BF16)

    return (w_ext, tile_row(g_qnorm[l], B_HEADS, b_scale), tile_row(g_knorm[l], B_HEADS), bd64,
            ln_g[l].reshape(1, -1), ln_b[l].reshape(1, -1), wsp, bsp,
            g_q_a[l].reshape(1, -1), wuq, g_kv_a[l].reshape(1, -1), wukvk, wukvv)


def _ffn_weights(w_in, w_out):
    return w_in.astype(BF16), w_out.astype(BF16)


def _lambda_init(layer_idx):
    return 0.8 - 0.6 * math.exp(-0.3 * layer_idx)


def kernel(x, c, ctx, c_ctx, w_ada, b_ada, g_pre, g_post, w_ffn1_in, w_ffn1_out, w_ffn2_in,
           w_ffn2_out, w_in, w_out, lam_vecs, g_subln, g_qnorm, g_knorm, w_spatial, b_spatial,
           ln_g, ln_b, g_q_a, w_uq, g_kv_a, w_ukv):
    n_batch, seq, d = x.shape
    n_ctx = ctx.shape[1]
    depth = w_ada.shape[0]

    n_cond = n_batch + 1
    pad = (-n_cond) % 8
    cc = jnp.concatenate([c, c_ctx[None, :], jnp.zeros((pad, d), c.dtype)], axis=0)
    mod = _ada(cc, w_ada, b_ada)
    mod_lat = mod[:, :n_batch].reshape(depth, n_batch, N_MOD, d)
    mod_ctx = mod[:, n_batch:n_cond].reshape(depth, 1, N_MOD, d)

    tables = _rope_tables(seq)
    x_lat = x.reshape(n_batch * seq, d)
    x_ctx = ctx.reshape(n_batch * n_ctx, d)
    lat = dict(rows_per_mod=seq)
    ctxk = dict(rows_per_mod=n_batch * n_ctx)

    for l in range(depth):
        need_ctx = l < depth - 1
        ffn1 = _ffn_weights(w_ffn1_in[l], w_ffn1_out[l])
        ffn2 = _ffn_weights(w_ffn2_in[l], w_ffn2_out[l])
        params = _layer_params(l, w_in, g_qnorm, g_knorm, w_spatial, b_spatial, ln_g, ln_b,
                               g_q_a, w_uq, g_kv_a, w_ukv)
        wo = w_out[l].reshape(N_MIXERS, -1, d).astype(BF16)
        a_extra = (lam_vecs[l], g_subln[l].reshape(-1, 1))
        lam0 = _lambda_init(l)

        x_ctx = _ffn(x_ctx, mod_ctx[l], g_pre[l], g_post[l], ffn1, j=0, **ctxk)
        x_lat = _ffn(x_lat, mod_lat[l], g_pre[l], g_post[l], ffn1, j=0, **lat)

        pc = _mix_in(x_ctx, mod_ctx[l], g_pre[l], params, None, seq=n_ctx, **ctxk)
        pl_ = _mix_in(x_lat, mod_lat[l], g_pre[l], params, tables, seq=seq, **lat)
        qa_c, ka_c, va_c, qb_c, kb_c, vb_c, cc_c, qd_c, kd_c, vd_c = pc
        qa_l, ka_l, va_l, qb_l, kb_l, vb_l, cc_l, qd_l, kd_l, vd_l = pl_

        a_l = _attn("a", qa_l, [(ka_l, va_l), (ka_c, va_c)], a_extra, n_batch=n_batch, lam_init=lam0)
        b_l = _attn("b", qb_l, [(kb_l, vb_l), (kb_c, vb_c)], (), n_batch=n_batch)
        d_l = _attn("d", qd_l, [(kd_l, vd_l), (kd_c, vd_c)], (), n_batch=n_batch)
        x_lat = _ffn(x_lat, mod_lat[l], g_pre[l], g_post[l], ffn2, j=2,
                     mixer=((a_l, b_l, cc_l, d_l), wo), **lat)

        if need_ctx:
            a_c = _attn("a", qa_c, [(ka_c, va_c)], a_extra, n_batch=n_batch, lam_init=lam0)
            b_c = _attn("b", qb_c, [(kb_c, vb_c)], (), n_batch=n_batch)
            d_c = _attn("d", qd_c, [(kd_c, vd_c)], (), n_batch=n_batch)
            x_ctx = _ffn(x_ctx, mod_ctx[l], g_pre[l], g_post[l], ffn2, j=2,
                         mixer=((a_c, b_c, cc_c, d_c), wo), **ctxk)

    return x_lat.reshape(n_batch, seq, d)
```

```python
import functools
import math

import jax
import jax.numpy as jnp
import numpy as np
from jax import lax
from jax.experimental import pallas as pl
from jax.experimental.pallas import tpu as pltpu

F32 = jnp.float32
BF16 = jnp.bfloat16

GRID_W = 64
CHUNK = 128
ROPE_THETA = 10000.0
NORM_EPS = 1e-6
FFN_RES = 0.5
N_MOD = 9
N_MIXERS = 4
B_HEADS, B_KV_HEADS, C_GROUPS, D_HEADS = 4, 2, 4, 4

LANES = 128
MXU_DIM = 256
VMEM_LIMIT_BYTES = 62 * 2**20

ROW_TILE = 1024
FF_TILE = 1024
MIX_SUB_ROWS = 1024


def _dot(a, b):
    return jnp.dot(a, b, preferred_element_type=F32)


def _dot_nt(a, b):
    return lax.dot_general(a, b, (((1,), (1,)), ((), ())), preferred_element_type=F32)


def _split_bf16(x):
    hi = x.astype(BF16)
    lo = (x - hi.astype(F32)).astype(BF16)
    return hi, lo


def _group_mean(xsq, bd_ref):
    hi, lo = _split_bf16(xsq)
    bd = bd_ref[...]
    return _dot(hi, bd) + _dot(lo, bd)


def _rms(x):
    return x * lax.rsqrt(jnp.mean(x * x, axis=-1, keepdims=True) + NORM_EPS)


def _pre_norm(x, mod_ref, gpre_ref, j):
    shift = mod_ref[0, 3 * j:3 * j + 1, :]
    gain = gpre_ref[j:j + 1, :] * (1.0 + mod_ref[0, 3 * j + 1:3 * j + 2, :])
    return _rms(x) * gain + shift


def _post_norm(x, y, mod_ref, gpost_ref, j, res):
    gain = (res * mod_ref[0, 3 * j + 2:3 * j + 3, :]) * gpost_ref[j:j + 1, :]
    return x + _rms(y) * gain


def _gelu_tanh(x):
    c = math.sqrt(2.0 / math.pi)
    return x * (0.5 * (1.0 + jnp.tanh(c * (x + 0.044715 * (x * x * x)))))


def _compiler_params(n_axes):
    return pltpu.CompilerParams(dimension_semantics=("arbitrary",) * n_axes,
                                vmem_limit_bytes=VMEM_LIMIT_BYTES)


def _resident(shape):
    nd = len(shape)
    return pl.BlockSpec(shape, lambda *_: (0,) * nd, pipeline_mode=pl.Buffered(1))


def _ada_kernel(c_ref, w_ref, b_ref, o_ref):
    c = c_ref[...]
    s = c * jax.nn.sigmoid(c)
    s_hi, s_lo = _split_bf16(s)
    w_hi, w_lo = _split_bf16(w_ref[0])
    o_ref[0] = _dot(s_hi, w_hi) + (_dot(s_hi, w_lo) + _dot(s_lo, w_hi)) + b_ref[0]


def _ada(cc, w_ada, b_ada):
    n_layers, d, n = w_ada.shape
    rows = cc.shape[0]
    tn = 1024
    return pl.pallas_call(
        _ada_kernel,
        grid=(n_layers, n // tn),
        in_specs=[pl.BlockSpec((rows, d), lambda l, j: (0, 0)),
                  pl.BlockSpec((1, d, tn), lambda l, j: (l, 0, j)),
                  pl.BlockSpec((1, 1, tn), lambda l, j: (l, 0, j))],
        out_specs=pl.BlockSpec((1, rows, tn), lambda l, j: (l, 0, j)),
        out_shape=jax.ShapeDtypeStruct((n_layers, rows, n), F32),
        compiler_params=_compiler_params(2),
        name="ada",
    )(cc, w_ada, b_ada.reshape(n_layers, 1, n))


def _ffn_kernel(*refs, j, n_mix):
    x_ref, mod_ref, gpre_ref, gpost_ref, w1_ref, w2_ref = refs[:6]
    mix_refs = refs[6:6 + n_mix]
    o_ref, xn_scr, acc_scr = refs[6 + n_mix:]
    d_ff = w2_ref.shape[0]

    def swiglu_chunk(xn, wg, wu, wd):
        hg = _dot(xn, wg)
        hu = _dot(xn, wu)
        return _dot(((hg * jax.nn.sigmoid(hg)) * hu).astype(BF16), wd)

    x = x_ref[...]
    if n_mix:
        wout_ref = mix_refs[-1]
        y = _dot(mix_refs[0][...], wout_ref[0])
        for g in range(1, n_mix - 1):
            y += _dot(mix_refs[g][...], wout_ref[g])
        x = _post_norm(x, y, mod_ref, gpost_ref, 1, 1.0)

    first = d_ff % FF_TILE or FF_TILE
    bounds = [0] + list(range(first, d_ff + 1, FF_TILE))
    chunks = list(zip(bounds[:-1], bounds[1:]))
    n_rows = x.shape[0]
    halves = [slice(0, n_rows // 2), slice(n_rows // 2, n_rows)]

    def chunk(rows, a, b):
        return swiglu_chunk(xn_scr[rows, :], w1_ref[:, a:b], w1_ref[:, d_ff + a:d_ff + b],
                            w2_ref[a:b, :])

    for rows in halves:
        xn_scr[rows, :] = _pre_norm(x[rows], mod_ref, gpre_ref, j).astype(BF16)
        acc_scr[rows, :] = chunk(rows, *chunks[0])
    for a, b in chunks[1:-1]:
        acc_scr[...] += chunk(slice(None), a, b)
    for rows in halves:
        y = acc_scr[rows, :] + chunk(rows, *chunks[-1]) if len(chunks) > 1 else acc_scr[rows, :]
        o_ref[rows, :] = _post_norm(x[rows], y, mod_ref, gpost_ref, j, FFN_RES)


def _ffn(x, mod, g_pre, g_post, weights, *, j, rows_per_mod, mixer=None):
    rows, d = x.shape
    tm = min(ROW_TILE, rows_per_mod)
    per_mod = rows_per_mod // tm
    mix_args, mix_specs = [], []
    if mixer is not None:
        groups, w_out = mixer
        mix_args = list(groups) + [w_out]
        mix_specs = ([pl.BlockSpec((tm, g.shape[1]), lambda i: (i, 0)) for g in groups]
                     + [_resident(w_out.shape)])
    return pl.pallas_call(
        functools.partial(_ffn_kernel, j=j, n_mix=len(mix_args)),
        grid=(rows // tm,),
        in_specs=[pl.BlockSpec((tm, d), lambda i: (i, 0)),
                  pl.BlockSpec((1, N_MOD, d), lambda i: (i // per_mod, 0, 0)),
                  _resident(g_pre.shape), _resident(g_post.shape)]
                 + [_resident(w.shape) for w in weights] + mix_specs,
        out_specs=pl.BlockSpec((tm, d), lambda i: (i, 0)),
        out_shape=jax.ShapeDtypeStruct((rows, d), F32),
        scratch_shapes=[pltpu.VMEM((tm, d), BF16), pltpu.VMEM((tm, d), F32)],
        compiler_params=_compiler_params(1),
        name=f"ffn{j}" if mixer is None else f"mix_out_ffn{j}",
    )(x, mod, g_pre, g_post, *weights, *mix_args)


_IN_BLOCKS = ("aq", "ak", "av", "bq", "bk", "bv", "cu", "cv", "dq", "dkr", "dkv")
_IN_WIDTHS = dict(aq=256, ak=256, av=256, bq=256, bk=256, bv=256, cu=256, cv=256,
                  dq=256, dkr=128, dkv=128)
_IN_OFFSETS = {}
_off = 0
for _n in _IN_BLOCKS:
    _IN_OFFSETS[_n] = _off
    _off += _IN_WIDTHS[_n]


def _rope_slabs(x, tabs, shift, store):
    cos, sin = tabs
    if cos is not None:
        lane = lax.broadcasted_iota(jnp.int32, cos.shape, 1)
        first = (lane & (2 * shift - 1)) < shift
    for s in range(x.shape[1] // LANES):
        xs = x[:, s * LANES:(s + 1) * LANES]
        if cos is not None:
            partner = jnp.where(first, pltpu.roll(xs, LANES - shift, axis=1),
                                pltpu.roll(xs, shift, axis=1))
            xs = xs * cos + partner * sin
        store(s, xs.astype(BF16))


def _mix_in_kernel(*refs, rope):
    (x_ref, mod_ref, gpre_ref, win_ref, gq_ref, gk_ref, bd64_ref, lng_ref, lnb_ref, wsp_ref,
     bsp_ref, gqa_ref, wuq_ref, gkva_ref, wukvk_ref, wukvv_ref) = refs[:16]
    refs = refs[16:]
    tab_refs = refs[:6] if rope else ()
    refs = refs[6:] if rope else refs
    qa_ref, ka_ref, va_ref, qb_ref, kb_ref, vb_ref, cc_ref, qd_ref, kd_ref, vd_ref = refs
    sub = min(MIX_SUB_ROWS, x_ref.shape[0])

    def sub_block(r, carry):
        rows = pl.ds(pl.multiple_of(r * sub, sub), sub)
        if rope:
            tabs = [t[rows, :] for t in tab_refs]
            t32, t64, td = tabs[0:2], tabs[2:4], tabs[4:6]
        else:
            t32 = t64 = td = (None, None)

        xn = _pre_norm(x_ref[rows, :], mod_ref, gpre_ref, 1).astype(BF16)

        def proj(name):
            o = _IN_OFFSETS[name]
            return _dot(xn, win_ref[:, o:o + _IN_WIDTHS[name]])

        def slab_store(ref):
            def store(s, v):
                ref[rows, s * LANES:(s + 1) * LANES] = v
            return store

        _rope_slabs(proj("aq"), t32, 8, slab_store(qa_ref))
        _rope_slabs(proj("ak"), t32, 8, slab_store(ka_ref))
        va_ref[rows, :] = proj("av").astype(BF16)

        def qk_norm(p, g_ref):
            return p * lax.rsqrt(_group_mean(p * p, bd64_ref) + NORM_EPS) * g_ref[...]

        _rope_slabs(qk_norm(proj("bq"), gq_ref), t64, 16, slab_store(qb_ref))
        _rope_slabs(qk_norm(proj("bk"), gk_ref), t64, 16, slab_store(kb_ref))
        vb_ref[rows, :] = proj("bv").astype(BF16)

        u = _gelu_tanh(proj("cu"))
        v = _gelu_tanh(proj("cv"))
        mu = jnp.mean(v, axis=-1, keepdims=True)
        vc = v - mu
        vln = (vc * lax.rsqrt(jnp.mean(vc * vc, axis=-1, keepdims=True) + NORM_EPS) * lng_ref[...]
               + lnb_ref[...]).astype(BF16)
        width = vln.shape[1]
        gw = width // C_GROUPS
        lane = lax.broadcasted_iota(jnp.int32, (CHUNK, width), 1)
        wsp = wsp_ref[...]
        bsp = bsp_ref[...]
        gated = []
        for ci in range(sub // CHUNK):
            crows = slice(ci * CHUNK, (ci + 1) * CHUNK)
            allg = _dot(wsp, vln[crows, :])
            mixed = allg[0:CHUNK]
            for g in range(1, C_GROUPS):
                mixed = jnp.where(lane >= g * gw, allg[g * CHUNK:(g + 1) * CHUNK], mixed)
            gated.append((u[crows, :] * (mixed + bsp)).astype(BF16))
        cc_ref[rows, :] = jnp.concatenate(gated, axis=0)

        cq = (_rms(proj("dq")) * gqa_ref[...]).astype(BF16)
        _rope_slabs(_dot(cq, wuq_ref[...]), td, 8, slab_store(qd_ref))
        ckv = (_rms(proj("dkv")) * gkva_ref[...]).astype(BF16)
        kr = proj("dkr")
        _rope_slabs(_dot(ckv, wukvk_ref[...]) + jnp.concatenate([kr] * D_HEADS, axis=1), td, 8,
                    slab_store(kd_ref))
        vd_ref[rows, :] = _dot(ckv, wukvv_ref[...]).astype(BF16)
        return carry

    lax.fori_loop(0, x_ref.shape[0] // sub, sub_block, 0)


def _mix_in(x, mod, g_pre, params, tables, *, rows_per_mod, seq):
    rows, d = x.shape
    tm = min(ROW_TILE, rows_per_mod)
    per_mod = rows_per_mod // tm
    rope = tables is not None
    n_mod_blocks = rows // rows_per_mod
    if rope:
        grid = (seq // tm, n_mod_blocks)
        row_map = lambda t, b: (b * per_mod + t, 0)
        mod_map = lambda t, b: (b, 0, 0)
        tab_specs = [pl.BlockSpec((tm, LANES), lambda t, b: (t, 0)) for _ in tables]
    else:
        grid = (rows // tm, 1)
        row_map = lambda i, _: (i, 0)
        mod_map = lambda i, _: (i // per_mod, 0, 0)
        tab_specs = []
    out_widths = (256, 256, 256, 256, 256, 256, 256, 512, 512, 256)
    return pl.pallas_call(
        functools.partial(_mix_in_kernel, rope=rope),
        grid=grid,
        in_specs=[pl.BlockSpec((tm, d), row_map), pl.BlockSpec((1, N_MOD, d), mod_map),
                  _resident(g_pre.shape)] + [_resident(p.shape) for p in params] + tab_specs,
        out_specs=[pl.BlockSpec((tm, w), row_map) for w in out_widths],
        out_shape=[jax.ShapeDtypeStruct((rows, w), BF16) for w in out_widths],
        compiler_params=_compiler_params(2),
        name="mix_in_rope" if rope else "mix_in",
    )(x, mod, g_pre, *params, *(tables or ()))


def _attn_groups(kind):
    if kind == "a":
        return [(0, [(h * 64 + m * 32, h * 64 + (m + 1) * 32, h) for h in (2 * g, 2 * g + 1)
                     for m in (0, 1)]) for g in range(2)]
    if kind == "b":
        return [(0, [(h * 64, (h + 1) * 64, h) for h in (2 * g, 2 * g + 1)]) for g in range(2)]
    if kind == "d":
        return [(g, [(i * LANES, (i + 1) * LANES, 2 * g + i) for i in range(2)]) for g in range(2)]
    raise ValueError(kind)


ATTN_Q_TILE = dict(a=256, b=512, d=512)
HEAD_W = 64
VT_ROWS = 80
ATTN_KEY_CHUNKS = 2


def _key_chunks(n_keys):
    tiles, ragged = divmod(n_keys, MXU_DIM)
    if ragged or tiles < ATTN_KEY_CHUNKS:
        half = n_keys // 2
        return [(0, half), (half, n_keys)]
    base, extra = divmod(tiles, ATTN_KEY_CHUNKS)
    bounds = [0]
    for c in range(ATTN_KEY_CHUNKS):
        bounds.append(bounds[-1] + (base + (c < extra)) * MXU_DIM)
    return list(zip(bounds[:-1], bounds[1:]))


def _attn_kernel(*refs, groups, n_src, tq, pair, lam_init):
    q_ref = refs[0]
    src = refs[1:1 + 2 * n_src]
    refs = refs[1 + 2 * n_src:]
    if pair:
        lam_ref, gsub_ref = refs[:2]
        refs = refs[2:]
    o_ref, vt_scr, s_a, s_b = refs

    off = 0
    k_spans = []
    for i in range(n_src):
        n = src[2 * i].shape[0]
        k_spans.append((off, off + n, src[2 * i]))
        for r in range(0, n, MXU_DIM):
            blk = src[2 * i + 1][r:r + MXU_DIM, :].astype(F32)
            blk_t = blk.T.astype(BF16)
            for h in range(blk_t.shape[0] // HEAD_W):
                vt_scr[h * VT_ROWS:h * VT_ROWS + HEAD_W, off + r:off + r + blk.shape[0]] = (
                    blk_t[h * HEAD_W:(h + 1) * HEAD_W])
        off += n
    n_keys = off
    for h in range(vt_scr.shape[0] // VT_ROWS):
        vt_scr[h * VT_ROWS + HEAD_W:(h + 1) * VT_ROWS, :] = jnp.ones((VT_ROWS - HEAD_W, n_keys), BF16)
    chunks = [slice(a, b) for a, b in _key_chunks(n_keys)]
    bufs = (s_a, s_b)
    n_tiles = q_ref.shape[0] // tq
    lane_q = lax.broadcasted_iota(jnp.int32, (tq, MXU_DIM), 1)

    if pair:
        lv = lam_ref[...]
        lam = (jnp.exp(jnp.sum(lv[0:1] * lv[1:2], axis=-1, keepdims=True))
               - jnp.exp(jnp.sum(lv[2:3] * lv[3:4], axis=-1, keepdims=True)) + lam_init)

    def tile_rows(i):
        if isinstance(i, int):
            return slice(i * tq, (i + 1) * tq)
        return pl.ds(pl.multiple_of(i * tq, tq), tq)

    def scores_t(group, i, k_rows, s_dst):
        slab, maps = group
        qcols = slice(slab * MXU_DIM, (slab + 1) * MXU_DIM)
        qg = q_ref[tile_rows(i), qcols]
        zero = jnp.zeros_like(qg)
        qs = jnp.concatenate([jnp.where((lane_q >= lo) & (lane_q < hi), qg, zero)
                              for lo, hi, _ in maps], axis=0)
        for lo, hi, k_ref in k_spans:
            a, b = max(k_rows.start, lo), min(k_rows.stop, hi)
            if a < b:
                s_dst[a - k_rows.start:b - k_rows.start, :] = _dot_nt(k_ref[a - lo:b - lo, qcols], qs)

    def q_tile(group, i, prefetch):
        _, maps = group
        heads = sorted({h for _, _, h in maps})

        def weigh(s, m, k_cols):
            pb = jnp.exp2(s - m).astype(BF16)
            return [_dot(vt_scr[h * VT_ROWS:(h + 1) * VT_ROWS, k_cols], pb[:, j * tq:(j + 1) * tq])
                    for j, (_, _, h) in enumerate(maps)]

        m = acc = None
        for c, k_cols in enumerate(chunks):
            if c + 1 < len(chunks):
                nxt = chunks[c + 1]
                scores_t(group, i, nxt, bufs[(c + 1) % 2].at[:nxt.stop - nxt.start])
            elif prefetch is not None:
                prefetch()
            s = bufs[c % 2][:k_cols.stop - k_cols.start, :]
            cm = jnp.max(s, axis=0, keepdims=True)
            if m is None:
                m = cm
                acc = weigh(s, m, k_cols)
            else:
                m_new = jnp.maximum(m, cm)
                alpha = jnp.exp2(m - m_new)
                acc_c = weigh(s, m_new, k_cols)
                acc = [a * alpha[:, j * tq:(j + 1) * tq] + a_c
                       for j, (a, a_c) in enumerate(zip(acc, acc_c))]
                m = m_new

        def out_t(j):
            return acc[j][:HEAD_W] * (1.0 / acc[j][HEAD_W:HEAD_W + 1])

        rows = []
        for h in heads:
            js = [j for j, (_, _, hh) in enumerate(maps) if hh == h]
            if pair:
                o = out_t(js[0]) - lam * out_t(js[1])
                o = (o * lax.rsqrt(jnp.mean(o * o, axis=0, keepdims=True) + NORM_EPS)
                     * gsub_ref[...] * (1.0 - lam_init))
            else:
                o = out_t(js[0])
            rows.append(o)
        out = jnp.concatenate(rows, axis=0).T
        ocols = slice(heads[0] * HEAD_W, (heads[-1] + 1) * HEAD_W)
        o_ref[tile_rows(i), ocols] = out.astype(BF16)

    first = s_a.at[:chunks[0].stop]
    scores_t(groups[0], 0, chunks[0], first)
    for gi, group in enumerate(groups):
        def body(i, carry, group=group):
            q_tile(group, i, lambda: scores_t(group, i + 1, chunks[0], first))
            return carry

        if n_tiles > 1:
            lax.fori_loop(0, n_tiles - 1, body, 0)
        if gi + 1 < len(groups):
            q_tile(group, n_tiles - 1,
                   lambda nxt=groups[gi + 1]: scores_t(nxt, 0, chunks[0], first))
        else:
            q_tile(group, n_tiles - 1, None)


def _attn(kind, q, sources, extra, *, n_batch, lam_init=0.0):
    groups = _attn_groups(kind)
    pair = kind == "a"
    rows, qw = q.shape
    sq = rows // n_batch
    tq = min(ATTN_Q_TILE[kind], sq)
    n_keys = sum(k.shape[0] // n_batch for k, _ in sources)
    out_w = sources[0][1].shape[1]
    cols = max(len(maps) for _, maps in groups) * tq
    in_specs = [pl.BlockSpec((sq, qw), lambda b: (b, 0))]
    args = [q]
    for k, v in sources:
        n = k.shape[0] // n_batch
        in_specs += [pl.BlockSpec((n, k.shape[1]), lambda b: (b, 0)),
                     pl.BlockSpec((n, v.shape[1]), lambda b: (b, 0))]
        args += [k, v]
    if pair:
        in_specs += [_resident(e.shape) for e in extra]
        args += list(extra)
    assert n_keys % (2 * LANES) == 0 and sq % tq == 0
    sizes = [b - a for a, b in _key_chunks(n_keys)]
    return pl.pallas_call(
        functools.partial(_attn_kernel, groups=groups, n_src=len(sources), tq=tq, pair=pair,
                          lam_init=lam_init),
        grid=(n_batch,),
        in_specs=in_specs,
        out_specs=pl.BlockSpec((sq, out_w), lambda b: (b, 0)),
        out_shape=jax.ShapeDtypeStruct((rows, out_w), BF16),
        scratch_shapes=[pltpu.VMEM((out_w // HEAD_W * VT_ROWS, n_keys), BF16),
                        pltpu.VMEM((max(sizes[0::2]), cols), F32),
                        pltpu.VMEM((max(sizes[1::2]), cols), F32)],
        compiler_params=_compiler_params(1),
        name=f"attn_{kind}_{len(sources)}",
    )(*args)


def _rope_table(seq, pattern):
    n_rows = seq // GRID_W
    n_freq = next(e[2] for e in pattern if e is not None)
    axis_dim = 2 * n_freq
    inv_freq = ROPE_THETA ** (-jnp.arange(0, axis_dim, 2, dtype=F32) / axis_dim)
    active = np.array([e is not None for e in pattern])
    axis = np.array([e[0] if e is not None else 0 for e in pattern])
    freq = np.array([e[1] if e is not None else 0 for e in pattern])
    first = np.array([bool(e[3]) if e is not None else False for e in pattern])

    def per_position(n, fn):
        return fn(jnp.arange(n, dtype=F32)[:, None] * inv_freq[freq][None, :])

    def full(fn):
        by_row = jnp.repeat(per_position(n_rows, fn), GRID_W, axis=0)
        by_col = jnp.tile(per_position(GRID_W, fn), (n_rows, 1))
        return jnp.where(axis[None, :] == 0, by_row, by_col)

    cos = jnp.where(active[None, :], full(jnp.cos), 1.0)
    sin = jnp.where(active[None, :], full(jnp.sin), 0.0)
    return cos, jnp.where(first[None, :], -sin, sin)


def _rope_pattern(rot_dim):
    n_freq = rot_dim // 4
    return [(r // (2 * n_freq), r % n_freq, n_freq, (r % (2 * n_freq)) < n_freq)
            for r in range(rot_dim)]


def _rope_tables(seq):
    p32 = _rope_pattern(32) * (LANES // 32)
    p64 = _rope_pattern(64) * (LANES // 64)
    pd = [None] * 64 + _rope_pattern(32) + [None] * 32
    return _rope_table(seq, p32) + _rope_table(seq, p64) + _rope_table(seq, pd)


def _layer_params(l, w_in, g_qnorm, g_knorm, w_spatial, b_spatial, ln_g, ln_b, g_q_a, w_uq,
                  g_kv_a, w_ukv):
    d_model = w_in.shape[1]
    sizes = (256, 256, 256, 256, 128, 128, 256, 256, 256, 128, 32)
    offs = [0]
    for s in sizes:
        offs.append(offs[-1] + s)
    aq, ak, av, bq, bk, bv, cu, cv, dq, dkv, dkr = (w_in[l][:, offs[i]:offs[i + 1]]
                                                    for i in range(11))
    log2e = math.log2(math.e)
    a_scale = 32 ** -0.5 * log2e
    b_scale = 64 ** -0.5 * log2e
    d_scale = 96 ** -0.5 * log2e

    def rep_kv(w):
        hd = w.shape[1] // B_KV_HEADS
        n_rep = B_HEADS // B_KV_HEADS
        return jnp.concatenate([w[:, (h // n_rep) * hd:(h // n_rep + 1) * hd]
                                for h in range(B_HEADS)], axis=1)

    zeros = lambda n: jnp.zeros((d_model, n), w_in.dtype)
    dkr_placed = jnp.concatenate([zeros(64), dkr, zeros(32)], axis=1)
    blocks = dict(aq=aq * a_scale, ak=ak, av=av, bq=bq, bk=rep_kv(bk), bv=rep_kv(bv), cu=cu, cv=cv,
                  dq=dq, dkr=dkr_placed, dkv=dkv)
    w_ext = jnp.concatenate([blocks[n] for n in _IN_BLOCKS], axis=1).astype(BF16)

    def tile_row(v, n, scale=1.0):
        return (jnp.tile(v, n) * scale).reshape(1, -1)

    bd64 = jnp.kron(jnp.eye(4, dtype=F32), jnp.full((64, 64), 1.0 / 64, F32)).astype(BF16)
    wsp = w_spatial[l].reshape(C_GROUPS * CHUNK, CHUNK).astype(BF16)
    bsp = jnp.repeat(b_spatial[l].T, 256 // C_GROUPS, axis=1)

    uq = w_uq[l].reshape(-1, D_HEADS, 96) * d_scale
    uq = jnp.concatenate([uq, jnp.zeros(uq.shape[:2] + (32,), uq.dtype)], axis=-1)
    wuq = uq.reshape(uq.shape[0], D_HEADS * LANES).astype(BF16)
    ukv = w_ukv[l].reshape(-1, D_HEADS, 128)
    ukv_k = jnp.concatenate([ukv[..., :64], jnp.zeros(ukv.shape[:2] + (64,), ukv.dtype)], axis=-1)
    wukvk = ukv_k.reshape(ukv.shape[0], D_HEADS * LANES).astype(BF16)
    wukvv = ukv[..., 64:].reshape(ukv.shape[0], D_HEADS * 64).astype(BF16)

    return (w_ext, tile_row(g_qnorm[l], B_HEADS, b_scale), tile_row(g_knorm[l], B_HEADS), bd64,
            ln_g[l].reshape(1, -1), ln_b[l].reshape(1, -1), wsp, bsp,
            g_q_a[l].reshape(1, -1), wuq, g_kv_a[l].reshape(1, -1), wukvk, wukvv)


def _ffn_weights(w_in, w_out):
    return w_in.astype(BF16), w_out.astype(BF16)


def _lambda_init(layer_idx):
    return 0.8 - 0.6 * math.exp(-0.3 * layer_idx)


def kernel(x, c, ctx, c_ctx, w_ada, b_ada, g_pre, g_post, w_ffn1_in, w_ffn1_out, w_ffn2_in,
           w_ffn2_out, w_in, w_out, lam_vecs, g_subln, g_qnorm, g_knorm, w_spatial, b_spatial,
           ln_g, ln_b, g_q_a, w_uq, g_kv_a, w_ukv):
    n_batch, seq, d = x.shape
    n_ctx = ctx.shape[1]
    depth = w_ada.shape[0]

    n_cond = n_batch + 1
    pad = (-n_cond) % 8
    cc = jnp.concatenate([c, c_ctx[None, :], jnp.zeros((pad, d), c.dtype)], axis=0)
    mod = _ada(cc, w_ada, b_ada)
    mod_lat = mod[:, :n_batch].reshape(depth, n_batch, N_MOD, d)
    mod_ctx = mod[:, n_batch:n_cond].reshape(depth, 1, N_MOD, d)

    tables = _rope_tables(seq)
    x_lat = x.reshape(n_batch * seq, d)
    x_ctx = ctx.reshape(n_batch * n_ctx, d)
    lat = dict(rows_per_mod=seq)
    ctxk = dict(rows_per_mod=n_batch * n_ctx)

    for l in range(depth):
        need_ctx = l < depth - 1
        ffn1 = _ffn_weights(w_ffn1_in[l], w_ffn1_out[l])
        ffn2 = _ffn_weights(w_ffn2_in[l], w_ffn2_out[l])
        params = _layer_params(l, w_in, g_qnorm, g_knorm, w_spatial, b_spatial, ln_g, ln_b,
                               g_q_a, w_uq, g_kv_a, w_ukv)
        wo = w_out[l].reshape(N_MIXERS, -1, d).astype(BF16)
        a_extra = (lam_vecs[l], g_subln[l].reshape(-1, 1))
        lam0 = _lambda_init(l)

        x_ctx = _ffn(x_ctx, mod_ctx[l], g_pre[l], g_post[l], ffn1, j=0, **ctxk)
        x_lat = _ffn(x_lat, mod_lat[l], g_pre[l], g_post[l], ffn1, j=0, **lat)

        pc = _mix_in(x_ctx, mod_ctx[l], g_pre[l], params, None, seq=n_ctx, **ctxk)
        pl_ = _mix_in(x_lat, mod_lat[l], g_pre[l], params, tables, seq=seq, **lat)
        qa_c, ka_c, va_c, qb_c, kb_c, vb_c, cc_c, qd_c, kd_c, vd_c = pc
        qa_l, ka_l, va_l, qb_l, kb_l, vb_l, cc_l, qd_l, kd_l, vd_l = pl_

        a_l = _attn("a", qa_l, [(ka_l, va_l), (ka_c, va_c)], a_extra, n_batch=n_batch, lam_init=lam0)
        b_l = _attn("b", qb_l, [(kb_l, vb_l), (kb_c, vb_c)], (), n_batch=n_batch)
        d_l = _attn("d", qd_l, [(kd_l, vd_l), (kd_c, vd_c)], (), n_batch=n_batch)
        x_lat = _ffn(x_lat, mod_lat[l], g_pre[l], g_post[l], ffn2, j=2,
                     mixer=((a_l, b_l, cc_l, d_l), wo), **lat)

        if need_ctx:
            a_c = _attn("a", qa_c, [(ka_c, va_c)], a_extra, n_batch=n_batch, lam_init=lam0)
            b_c = _attn("b", qb_c, [(kb_c, vb_c)], (), n_batch=n_batch)
            d_c = _attn("d", qd_c, [(kd_c, vd_c)], (), n_batch=n_batch)
            x_ctx = _ffn(x_ctx, mod_ctx[l], g_pre[l], g_post[l], ffn2, j=2,
                         mixer=((a_c, b_c, cc_c, d_c), wo), **ctxk)

    return x_lat.reshape(n_batch, seq, d)
```
